```python
import math
import jax
import jax.numpy as jnp
from jax import lax
import numpy as np

D_MODEL = 1024
BATCH = 2
SEQ = 8192
DEPTH = 4
DEC_BATCH = 128
DEC_SEQ = 8
PAST_LEN = 8192
PAGE_SIZE = 128

N_EVEN = (DEPTH + 1) // 2
N_ODD = DEPTH // 2
ATT_HEADS = 8
ATT_KV = 2
ATT_GROUP = ATT_HEADS // ATT_KV
N_BUCKETS = 32
MAX_DIST = 128
WINDOW = 128
HD_A = 64
QBLK = 128
SSM_HEADS = 8
SSM_P = 64
SSM_INNER = SSM_HEADS * SSM_P
SSM_GROUPS = 2
SSM_N = 64
CONV_W = 4
CONV_DIM = SSM_INNER + 2 * SSM_GROUPS * SSM_N
SSD_CHUNK = 128
HD_C = 128
MOBA_BLOCK = 256
MOBA_TOPK = 3
MOBA_QBLK = 32
MEM_LEN = 256
MEM_HEADS = 4
MEM_HD = 64
MEM_W = MEM_HEADS * MEM_HD
D_FF = -(-8 * D_MODEL // (3 * 256)) * 256
E_Q = ATT_HEADS * HD_A
E_KV = ATT_KV * HD_A
EVEN_SPLITS = (E_Q, E_Q + E_KV, E_Q + 2 * E_KV, E_Q + 2 * E_KV + SSM_INNER, E_Q + 2 * E_KV + SSM_INNER + CONV_DIM)
EVEN_IN = EVEN_SPLITS[-1] + SSM_HEADS
EVEN_MIX = E_Q + SSM_INNER
C_Q = ATT_HEADS * HD_C
C_KV = ATT_KV * HD_C
ODD_IN = C_Q + 2 * C_KV
EPS = 1e-6
NEG = -1e30

kernel_name = 'hybrid_swa_ssd_moba_decoder_step'


def rmsnorm(x, w):
    xf = x.astype(jnp.float32)
    y = xf * lax.rsqrt(jnp.mean(xf * xf, axis=-1, keepdims=True) + EPS)
    return (y * w.astype(jnp.float32)).astype(x.dtype)


def rel_bucket(dist):
    n = jnp.maximum(dist, 0)
    exact = N_BUCKETS // 2
    nf = jnp.maximum(n, 1).astype(jnp.float32)
    large = exact + (jnp.log(nf / exact) / math.log(MAX_DIST / exact) * (N_BUCKETS - exact)).astype(jnp.int32)
    return jnp.where(n < exact, n, jnp.minimum(large, N_BUCKETS - 1))


def head_bias_table(rel_bias):
    return rel_bias.T.reshape(ATT_KV, ATT_GROUP, N_BUCKETS).astype(jnp.float32)


def sink_attend(q, k, v, bias, valid, sink):
    scale = q.shape[-1] ** -0.5
    s = jnp.einsum('...qhgd,...khd->...hgqk', q, k).astype(jnp.float32) * scale + bias
    s = jnp.where(valid, s, NEG)
    sk = sink[:, :, None, None]
    m = jnp.maximum(jnp.max(s, axis=-1, keepdims=True), sk)
    p = jnp.exp(s - m)
    w = p / (jnp.sum(p, axis=-1, keepdims=True) + jnp.exp(sk - m))
    return jnp.einsum('...hgqk,...khd->...qhgd', w.astype(v.dtype), v)


def ssd_scan(h0, x, dt, bm, cm, a):
    bsz, n = x.shape[:2]
    c = min(SSD_CHUNK, n)
    nc = n // c
    hg = jnp.arange(SSM_HEADS) // (SSM_HEADS // SSM_GROUPS)

    def chunks(t):
        return jnp.moveaxis(t.reshape(bsz, nc, c, *t.shape[2:]), 1, 0)

    xs = (chunks(x.astype(jnp.float32)), chunks(dt), chunks(bm[:, :, hg].astype(jnp.float32)), chunks(cm[:, :, hg].astype(jnp.float32)))
    tri = jnp.tril(jnp.ones((c, c), dtype=bool))[None, :, :, None]

    def step(h, inp):
        xc, dtc, bc, cc = inp
        cs = jnp.cumsum(dtc * a, axis=1)
        seg = cs[:, :, None, :] - cs[:, None, :, :]
        decay = jnp.exp(jnp.where(tri, seg, -jnp.inf))
        xdt = xc * dtc[..., None]
        scores = jnp.einsum('bthn,bshn->btsh', cc, bc) * decay
        y = jnp.einsum('btsh,bshp->bthp', scores, xdt) + jnp.einsum('bthn,bhpn->bthp', cc, h) * jnp.exp(cs)[..., None]
        tail = jnp.exp(cs[:, -1:] - cs)
        h_new = h * jnp.exp(cs[:, -1])[:, :, None, None] + jnp.einsum('bshn,bshp->bhpn', bc * tail[..., None], xdt)
        return h_new, y

    h_last, ys = lax.scan(step, h0, xs)
    return jnp.moveaxis(ys, 0, 1).reshape(bsz, n, SSM_HEADS, SSM_P), h_last


def even_mixer(h, w_in, w_out, q_norm, k_norm, sinks, conv_w, conv_b, dt_bias, a_log, d_skip, gate_norm, rel_bias,
               win_k=None, win_v=None, conv_state=None, ssm_state=None):
    bsz, n = h.shape[:2]
    q, k, v, z, xbc, dt = jnp.split(h @ w_in, EVEN_SPLITS, axis=-1)
    q = rmsnorm(q.reshape(bsz, n, ATT_KV, ATT_GROUP, HD_A), q_norm)
    k = rmsnorm(k.reshape(bsz, n, ATT_KV, HD_A), k_norm)
    v = v.reshape(bsz, n, ATT_KV, HD_A)
    tab = head_bias_table(rel_bias)
    sink = sinks.reshape(ATT_KV, ATT_GROUP).astype(jnp.float32)
    if win_k is None:
        nq = n // QBLK
        qb = q.reshape(bsz, nq, QBLK, ATT_KV, ATT_GROUP, HD_A)
        kb = k.reshape(bsz, nq, QBLK, ATT_KV, HD_A)
        vb = v.reshape(bsz, nq, QBLK, ATT_KV, HD_A)
        zk = jnp.zeros_like(kb[:, :1])
        kband = jnp.concatenate([jnp.concatenate([zk, kb[:, :-1]], axis=1), kb], axis=2)
        vband = jnp.concatenate([jnp.concatenate([zk, vb[:, :-1]], axis=1), vb], axis=2)
        j = jnp.arange(2 * QBLK)[None, :]
        dist = QBLK + jnp.arange(QBLK)[:, None] - j
        blk = jnp.arange(nq)[:, None, None]
        valid = (dist >= 0) & (dist < WINDOW) & (blk * QBLK - QBLK + j >= 0)
        att = sink_attend(qb, kband, vband, tab[:, :, rel_bucket(dist)], valid[:, None, None], sink)
        new_k, new_v = k[:, -WINDOW:], v[:, -WINDOW:]
    else:
        nbuf = win_k.shape[1]
        kc = jnp.concatenate([win_k, k], axis=1)
        vc = jnp.concatenate([win_v, v], axis=1)
        dist = nbuf + jnp.arange(n)[:, None] - jnp.arange(nbuf + n)[None, :]
        valid = (dist >= 0) & (dist < WINDOW)
        att = sink_attend(q, kc, vc, tab[:, :, rel_bucket(dist)], valid, sink)
        new_k, new_v = kc[:, -nbuf:], vc[:, -nbuf:]
    att = att.reshape(bsz, n, E_Q)
    if conv_state is None:
        conv_state = jnp.zeros((bsz, CONV_W - 1, CONV_DIM), xbc.dtype)
    xp = jnp.concatenate([conv_state, xbc], axis=1)
    conv = sum(xp[:, i:i + n] * conv_w[i] for i in range(CONV_W)) + conv_b
    xbc_c = jax.nn.silu(conv)
    new_conv = xp[:, -(CONV_W - 1):]
    xs, bm, cm = jnp.split(xbc_c, (SSM_INNER, SSM_INNER + SSM_GROUPS * SSM_N), axis=-1)
    xs = xs.reshape(bsz, n, SSM_HEADS, SSM_P)
    bm = bm.reshape(bsz, n, SSM_GROUPS, SSM_N)
    cm = cm.reshape(bsz, n, SSM_GROUPS, SSM_N)
    dt = jax.nn.softplus(dt.astype(jnp.float32) + dt_bias.astype(jnp.float32))
    a = -jnp.exp(a_log.astype(jnp.float32))
    if ssm_state is None:
        h0 = jnp.zeros((bsz, SSM_HEADS, SSM_P, SSM_N), jnp.float32)
    else:
        h0 = ssm_state.astype(jnp.float32)
    y, h_last = ssd_scan(h0, xs, dt, bm, cm, a)
    y = y + xs.astype(jnp.float32) * d_skip.astype(jnp.float32)[:, None]
    y = y.reshape(bsz, n, SSM_INNER) * jax.nn.silu(z.astype(jnp.float32))
    y = rmsnorm(y.reshape(bsz, n, SSM_GROUPS, SSM_INNER // SSM_GROUPS), gate_norm.reshape(SSM_GROUPS, SSM_INNER // SSM_GROUPS))
    y = y.reshape(bsz, n, SSM_INNER).astype(h.dtype)
    out = jnp.concatenate([att.astype(h.dtype), y], axis=-1) @ w_out
    return out, (new_k, new_v, new_conv, h_last.astype(h.dtype))


def moba_attend(q, t, k_own, v_own, own_pos, tab, k_sel=None, v_sel=None, sel_pos=None, sel_valid=None):
    scale = q.shape[-1] ** -0.5
    dist_own = t[:, None] - own_pos[None, :]
    s_own = jnp.einsum('...hgqd,...hjd->...hgqj', q, k_own).astype(jnp.float32) * scale
    s_own = jnp.where(dist_own >= 0, s_own + tab[:, :, rel_bucket(dist_own)], NEG)
    if k_sel is None:
        p = jax.nn.softmax(s_own, axis=-1)
        return jnp.einsum('...hgqj,...hjd->...hgqd', p.astype(v_own.dtype), v_own)
    hi = jnp.arange(ATT_KV)[:, None, None, None, None]
    gi = jnp.arange(ATT_GROUP)[:, None, None, None]
    s_sel = jnp.einsum('...hgqd,...hgqkjd->...hgqkj', q, k_sel).astype(jnp.float32) * scale
    s_sel = jnp.where(sel_valid[..., None], s_sel + tab[hi, gi, rel_bucket(t[:, None, None] - sel_pos)], NEG)
    n_sel = s_sel.shape[-2] * s_sel.shape[-1]
    p = jax.nn.softmax(jnp.concatenate([s_sel.reshape(*s_sel.shape[:-2], n_sel), s_own], axis=-1), axis=-1)
    p_sel = p[..., :n_sel].reshape(s_sel.shape)
    p_own = p[..., n_sel:]
    return (jnp.einsum('...hgqkj,...hgqkjd->...hgqd', p_sel.astype(v_sel.dtype), v_sel)
            + jnp.einsum('...hgqj,...hjd->...hgqd', p_own.astype(v_own.dtype), v_own))


def moba_prompt(q, k, v, tab):
    bsz, n = q.shape[:2]
    nb = -(-n // MOBA_BLOCK)
    pad = ((0, 0), (0, nb * MOBA_BLOCK - n), (0, 0), (0, 0))
    kb = jnp.pad(k, pad).reshape(bsz, nb, MOBA_BLOCK, ATT_KV, HD_C).transpose(0, 3, 1, 2, 4)
    vb = jnp.pad(v, pad).reshape(bsz, nb, MOBA_BLOCK, ATT_KV, HD_C).transpose(0, 3, 1, 2, 4)
    kmean = jnp.mean(kb.astype(jnp.float32), axis=3).astype(k.dtype)
    nq = n // MOBA_QBLK
    qb = q.reshape(bsz, nq, MOBA_QBLK, ATT_KV, ATT_GROUP, HD_C).transpose(1, 0, 3, 4, 2, 5)
    k_top = min(MOBA_TOPK, nb - 1)
    bi = jnp.arange(bsz)[:, None, None, None, None]
    hi = jnp.arange(ATT_KV)[None, :, None, None, None]

    def one_block(args):
        qi, qblk = args
        cur = (qi * MOBA_QBLK) // MOBA_BLOCK
        t = qi * MOBA_QBLK + jnp.arange(MOBA_QBLK)
        k_own = lax.dynamic_index_in_dim(kb, cur, axis=2, keepdims=False)
        v_own = lax.dynamic_index_in_dim(vb, cur, axis=2, keepdims=False)
        own_pos = cur * MOBA_BLOCK + jnp.arange(MOBA_BLOCK)
        if k_top == 0:
            return moba_attend(qblk, t, k_own, v_own, own_pos, tab)
        gate = jnp.einsum('bhgqd,bhnd->bhgqn', qblk, kmean).astype(jnp.float32)
        gate = jnp.where(jnp.arange(nb) < cur, gate, NEG)
        _, idx = lax.top_k(gate, k_top)
        k_sel = kb[bi, hi, idx]
        v_sel = vb[bi, hi, idx]
        sel_pos = idx[..., None] * MOBA_BLOCK + jnp.arange(MOBA_BLOCK)
        return moba_attend(qblk, t, k_own, v_own, own_pos, tab, k_sel, v_sel, sel_pos, idx < cur)

    out = lax.map(one_block, (jnp.arange(nq), qb))
    return out.transpose(1, 0, 4, 2, 3, 5).reshape(bsz, n, C_Q)


def moba_sample(q, k, v, pool_k, pool_v, page_table, tab):
    bsz, n = q.shape[:2]
    ppb = MOBA_BLOCK // PAGE_SIZE
    nfp = PAST_LEN // MOBA_BLOCK
    own_past = PAST_LEN - nfp * MOBA_BLOCK
    n_own_pages = own_past // PAGE_SIZE
    k_top = min(MOBA_TOPK, nfp)
    t = PAST_LEN + jnp.arange(n)
    hi = jnp.arange(ATT_KV)[:, None, None, None, None]
    qs = q.transpose(0, 2, 3, 1, 4)
    ks = k.transpose(0, 2, 1, 3)
    vs = v.transpose(0, 2, 1, 3)

    def one_seq(args):
        qq, kn, vn, pt = args
        k_own, v_own = kn, vn
        if own_past > 0:
            own_pages = pt[nfp * ppb:nfp * ppb + n_own_pages]
            k_own = jnp.concatenate([pool_k[own_pages].reshape(own_past, ATT_KV, HD_C).transpose(1, 0, 2), kn], axis=1)
            v_own = jnp.concatenate([pool_v[own_pages].reshape(own_past, ATT_KV, HD_C).transpose(1, 0, 2), vn], axis=1)
        own_pos = nfp * MOBA_BLOCK + jnp.arange(own_past + n)
        if k_top == 0:
            return moba_attend(qq, t, k_own, v_own, own_pos, tab)
        past_rows = pool_k[pt[:nfp * ppb]].reshape(nfp, MOBA_BLOCK, ATT_KV, HD_C)
        kmean = jnp.mean(past_rows.astype(jnp.float32), axis=1).astype(kn.dtype)
        gate = jnp.einsum('hgqd,nhd->hgqn', qq, kmean).astype(jnp.float32)
        _, idx = lax.top_k(gate, k_top)
        phys = pt[idx[..., None] * ppb + jnp.arange(ppb)]
        k_sel = pool_k[phys, :, hi].reshape(*idx.shape, MOBA_BLOCK, HD_C)
        v_sel = pool_v[phys, :, hi].reshape(*idx.shape, MOBA_BLOCK, HD_C)
        sel_pos = idx[..., None] * MOBA_BLOCK + jnp.arange(MOBA_BLOCK)
        return moba_attend(qq, t, k_own, v_own, own_pos, tab, k_sel, v_sel, sel_pos, idx < nfp)

    out = lax.map(one_seq, (qs, ks, vs, page_table))
    return out.transpose(0, 3, 1, 2, 4).reshape(bsz, n, C_Q)


def odd_mixer(h, w_in, w_out, q_norm, k_norm, rel_bias, pool_k=None, pool_v=None, page_table=None):
    bsz, n = h.shape[:2]
    q, k, v = jnp.split(h @ w_in, (C_Q, C_Q + C_KV), axis=-1)
    q = rmsnorm(q.reshape(bsz, n, ATT_KV, ATT_GROUP, HD_C), q_norm)
    k = rmsnorm(k.reshape(bsz, n, ATT_KV, HD_C), k_norm)
    v = v.reshape(bsz, n, ATT_KV, HD_C)
    tab = head_bias_table(rel_bias)
    if pool_k is None:
        att = moba_prompt(q, k, v, tab)
    else:
        att = moba_sample(q, k, v, pool_k, pool_v, page_table, tab)
    return att.astype(h.dtype) @ w_out, (k, v)


def memory_kv(mem, wk, wv, k_norm):
    bsz = mem.shape[0]
    mk = rmsnorm((mem @ wk).reshape(bsz, MEM_LEN, MEM_HEADS, MEM_HD), k_norm)
    mv = (mem @ wv).reshape(bsz, MEM_LEN, MEM_HEADS, MEM_HD)
    return mk, mv


def cross_attend(h, mk, mv, wq, wo, q_norm):
    bsz, n = h.shape[:2]
    q = rmsnorm((h @ wq).reshape(bsz, n, MEM_HEADS, MEM_HD), q_norm)
    s = jnp.einsum('blhd,bmhd->bhlm', q, mk).astype(jnp.float32) * (MEM_HD ** -0.5)
    p = jax.nn.softmax(s, axis=-1)
    o = jnp.einsum('bhlm,bmhd->blhd', p.astype(mv.dtype), mv).reshape(bsz, n, MEM_W)
    return o @ wo


def swiglu(h, w_gate, w_up, w_down):
    return (jax.nn.silu(h @ w_gate) * (h @ w_up)) @ w_down


def setup_inputs(seed: int = 0) -> dict:
    key = jax.random.key(seed)
    keys = iter(jax.random.split(key, 64))
    f32 = jnp.float32

    def nrm(shape, scale=1.0):
        return jax.random.normal(next(keys), shape, f32) * scale

    def gain(shape):
        return 1.0 + nrm(shape, 0.01)

    n_pages = PAST_LEN // PAGE_SIZE
    n_pool = (DEC_BATCH * n_pages * 5) // 4
    win = min(WINDOW, PAST_LEN)
    page_table = jax.random.permutation(next(keys), n_pool)[:DEC_BATCH * n_pages].reshape(DEC_BATCH, n_pages).astype(jnp.int32)
    dt0 = jnp.exp(jax.random.uniform(next(keys), (N_EVEN, SSM_HEADS), f32, math.log(1e-3), math.log(1e-1)))
    dt_bias = dt0 + jnp.log(-jnp.expm1(-dt0))
    a_log = jnp.log(jax.random.uniform(next(keys), (N_EVEN, SSM_HEADS), f32, 1.0, 16.0))
    return {
        'x_prompt': nrm((BATCH, SEQ, D_MODEL)),
        'x_sample': nrm((DEC_BATCH, DEC_SEQ, D_MODEL)),
        'cache_win_k': nrm((N_EVEN, DEC_BATCH, win, ATT_KV, HD_A)),
        'cache_win_v': nrm((N_EVEN, DEC_BATCH, win, ATT_KV, HD_A)),
        'state_conv': nrm((N_EVEN, DEC_BATCH, CONV_W - 1, CONV_DIM)),
        'state_ssm': nrm((N_EVEN, DEC_BATCH, SSM_HEADS, SSM_P, SSM_N), 0.5),
        'cache_moba_k': nrm((N_ODD, n_pool, PAGE_SIZE, ATT_KV, HD_C)),
        'cache_moba_v': nrm((N_ODD, n_pool, PAGE_SIZE, ATT_KV, HD_C)),
        'cache_mem_k': nrm((DEPTH, DEC_BATCH, MEM_LEN, MEM_HEADS, MEM_HD)),
        'cache_mem_v': nrm((DEPTH, DEC_BATCH, MEM_LEN, MEM_HEADS, MEM_HD)),
        'page_table': page_table,
        'mem_prompt': nrm((BATCH, MEM_LEN, D_MODEL)),
        'rel_bias': nrm((N_BUCKETS, ATT_HEADS), 0.5),
        'norm_mix': gain((DEPTH, D_MODEL)),
        'norm_mem': gain((DEPTH, D_MODEL)),
        'norm_ffn': gain((DEPTH, D_MODEL)),
        'even_w_in': nrm((N_EVEN, D_MODEL, EVEN_IN), D_MODEL ** -0.5),
        'even_w_out': nrm((N_EVEN, EVEN_MIX, D_MODEL), EVEN_MIX ** -0.5),
        'even_q_norm': gain((N_EVEN, HD_A)),
        'even_k_norm': gain((N_EVEN, HD_A)),
        'even_sinks': nrm((N_EVEN, ATT_HEADS), 0.5),
        'even_conv_w': nrm((N_EVEN, CONV_W, CONV_DIM), CONV_W ** -0.5),
        'even_conv_b': nrm((N_EVEN, CONV_DIM), 0.02),
        'even_dt_bias': dt_bias,
        'even_a_log': a_log,
        'even_d_skip': gain((N_EVEN, SSM_HEADS)),
        'even_gate_norm': gain((N_EVEN, SSM_INNER)),
        'odd_w_in': nrm((N_ODD, D_MODEL, ODD_IN), D_MODEL ** -0.5),
        'odd_w_out': nrm((N_ODD, C_Q, D_MODEL), C_Q ** -0.5),
        'odd_q_norm': gain((N_ODD, HD_C)),
        'odd_k_norm': gain((N_ODD, HD_C)),
        'mem_wq': nrm((DEPTH, D_MODEL, MEM_W), D_MODEL ** -0.5),
        'mem_wk': nrm((DEPTH, D_MODEL, MEM_W), D_MODEL ** -0.5),
        'mem_wv': nrm((DEPTH, D_MODEL, MEM_W), D_MODEL ** -0.5),
        'mem_wo': nrm((DEPTH, MEM_W, D_MODEL), MEM_W ** -0.5),
        'mem_q_norm': gain((DEPTH, MEM_HD)),
        'mem_k_norm': gain((DEPTH, MEM_HD)),
        'ffn_w_gate': nrm((DEPTH, D_MODEL, D_FF), D_MODEL ** -0.5),
        'ffn_w_up': nrm((DEPTH, D_MODEL, D_FF), D_MODEL ** -0.5),
        'ffn_w_down': nrm((DEPTH, D_FF, D_MODEL), D_FF ** -0.5),
    }


def reference(x_prompt, x_sample, cache_win_k, cache_win_v, state_conv, state_ssm, cache_moba_k, cache_moba_v,
              cache_mem_k, cache_mem_v, page_table, mem_prompt, rel_bias, norm_mix, norm_mem, norm_ffn,
              even_w_in, even_w_out, even_q_norm, even_k_norm, even_sinks, even_conv_w, even_conv_b,
              even_dt_bias, even_a_log, even_d_skip, even_gate_norm, odd_w_in, odd_w_out, odd_q_norm, odd_k_norm,
              mem_wq, mem_wk, mem_wv, mem_wo, mem_q_norm, mem_k_norm, ffn_w_gate, ffn_w_up, ffn_w_down):
    yp, ys = x_prompt, x_sample
    pwk, pwv, pcv, pss, pmk, pmv, pxk, pxv = [], [], [], [], [], [], [], []
    swk, swv, scv, sss, smk, smv = [], [], [], [], [], []
    for l in range(DEPTH):
        hp = rmsnorm(yp, norm_mix[l])
        hs = rmsnorm(ys, norm_mix[l])
        if l % 2 == 0:
            e = l // 2
            ew = (even_w_in[e], even_w_out[e], even_q_norm[e], even_k_norm[e], even_sinks[e], even_conv_w[e],
                  even_conv_b[e], even_dt_bias[e], even_a_log[e], even_d_skip[e], even_gate_norm[e], rel_bias)
            op, (wk_p, wv_p, cv_p, ss_p) = even_mixer(hp, *ew)
            osm, (wk_s, wv_s, cv_s, ss_s) = even_mixer(hs, *ew, cache_win_k[e], cache_win_v[e], state_conv[e], state_ssm[e])
            pwk.append(wk_p); pwv.append(wv_p); pcv.append(cv_p); pss.append(ss_p)
            swk.append(wk_s); swv.append(wv_s); scv.append(cv_s); sss.append(ss_s)
        else:
            o = l // 2
            ow = (odd_w_in[o], odd_w_out[o], odd_q_norm[o], odd_k_norm[o], rel_bias)
            op, (mk_p, mv_p) = odd_mixer(hp, *ow)
            osm, (mk_s, mv_s) = odd_mixer(hs, *ow, cache_moba_k[o], cache_moba_v[o], page_table)
            pmk.append(mk_p); pmv.append(mv_p)
            smk.append(mk_s); smv.append(mv_s)
        yp = yp + op
        ys = ys + osm
        xk, xv = memory_kv(mem_prompt, mem_wk[l], mem_wv[l], mem_k_norm[l])
        pxk.append(xk); pxv.append(xv)
        yp = yp + cross_attend(rmsnorm(yp, norm_mem[l]), xk, xv, mem_wq[l], mem_wo[l], mem_q_norm[l])
        ys = ys + cross_attend(rmsnorm(ys, norm_mem[l]), cache_mem_k[l], cache_mem_v[l], mem_wq[l], mem_wo[l], mem_q_norm[l])
        yp = yp + swiglu(rmsnorm(yp, norm_ffn[l]), ffn_w_gate[l], ffn_w_up[l], ffn_w_down[l])
        ys = ys + swiglu(rmsnorm(ys, norm_ffn[l]), ffn_w_gate[l], ffn_w_up[l], ffn_w_down[l])
    p_win_k, p_win_v, p_conv, p_ssm = jnp.stack(pwk), jnp.stack(pwv), jnp.stack(pcv), jnp.stack(pss)
    p_moba_k, p_moba_v, p_mem_k, p_mem_v = jnp.stack(pmk), jnp.stack(pmv), jnp.stack(pxk), jnp.stack(pxv)
    s_win_k, s_win_v, s_conv, s_ssm = jnp.stack(swk), jnp.stack(swv), jnp.stack(scv), jnp.stack(sss)
    s_moba_k, s_moba_v = jnp.stack(smk), jnp.stack(smv)
    return (yp, ys, p_win_k, p_win_v, p_conv, p_ssm, p_moba_k, p_moba_v, p_mem_k, p_mem_v,
            s_win_k, s_win_v, s_conv, s_ssm, s_moba_k, s_moba_v)
```

```python
import functools
import math

import jax
import jax.numpy as jnp
import numpy as np
from jax import lax
from jax.experimental import pallas as pl
from jax.experimental.pallas import tpu as pltpu

F32 = jnp.float32
BF16 = jnp.bfloat16

D_MODEL = 1024
ATT_HEADS = 8
ATT_KV = 2
ATT_GROUP = ATT_HEADS // ATT_KV
N_BUCKETS = 32
MAX_DIST = 128
WINDOW = 128
HD_A = 64
SSM_HEADS = 8
SSM_P = 64
SSM_INNER = SSM_HEADS * SSM_P
SSM_GROUPS = 2
SSM_N = 64
CONV_W = 4
CONV_DIM = SSM_INNER + 2 * SSM_GROUPS * SSM_N
SSD_CHUNK = 128
HD_C = 128
MOBA_BLOCK = 256
MOBA_TOPK = 3
PAGE_SIZE = 128
MEM_LEN = 256
MEM_HEADS = 4
MEM_HD = 64
MEM_W = MEM_HEADS * MEM_HD
E_Q = ATT_HEADS * HD_A
E_KV = ATT_KV * HD_A
C_Q = ATT_HEADS * HD_C
C_KV = ATT_KV * HD_C
EPS = 1e-6
NEG = -1e30
LOWEST = -3e38

LANES = 128
EVEN_W = E_Q + 2 * E_KV + SSM_INNER + CONV_DIM + LANES
ROW_TILE = 512
FFN_ROW_TILE = 1024
FFN_COL_TILE = 256
VMEM_LIMIT = 56 * 1024 * 1024


def _dot(a, b):
    return jnp.dot(a, b, preferred_element_type=F32)


def _dot_nt(a, b):
    return lax.dot_general(a, b, (((1,), (1,)), ((), ())), preferred_element_type=F32)


def _dot_tn(a, b):
    return lax.dot_general(a, b, (((0,), (0,)), ((), ())), preferred_element_type=F32)


def _split3(x):
    hi = x.astype(BF16)
    r = x - hi.astype(F32)
    mid = r.astype(BF16)
    lo = (r - mid.astype(F32)).astype(BF16)
    return hi, mid, lo


def _rms(x, w):
    ms = jnp.mean(x * x, axis=-1, keepdims=True)
    return x * lax.rsqrt(ms + EPS) * w


def _headnorm_bd(y, ones_bd, w, hd):
    sq = y * y
    hi = sq.astype(BF16)
    lo = (sq - hi.astype(F32)).astype(BF16)
    ss = _dot(hi, ones_bd) + _dot(lo, ones_bd)
    return y * lax.rsqrt(ss * (1.0 / hd) + EPS) * w


def _silu(x):
    return x * jax.nn.sigmoid(x)


def _softplus(x):
    return jnp.maximum(x, 0.0) + jnp.log1p(jnp.exp(-jnp.abs(x)))


def _bucket_np(dist):
    n = np.maximum(np.asarray(dist, np.int64), 0)
    nf = np.maximum(n, 1).astype(np.float64)
    exact = N_BUCKETS // 2
    large = exact + np.floor(np.log(nf / exact) / math.log(MAX_DIST / exact) * (N_BUCKETS - exact) + 1e-6).astype(np.int64)
    return np.where(n < exact, n, np.minimum(large, N_BUCKETS - 1)).astype(np.int32)


def _block_diag_ones(width, hd):
    i = np.arange(width) // hd
    return jnp.asarray((i[:, None] == i[None, :]).astype(np.float32), BF16)


def _top3_mask(gate, lane_f):
    sel = jnp.zeros_like(gate)
    for _ in range(MOBA_TOPK):
        mx = jnp.max(gate, axis=-1, keepdims=True)
        idx = jnp.min(jnp.where(gate == mx, lane_f, 1e9), axis=-1, keepdims=True)
        hit = lane_f == idx
        sel = jnp.where(hit, 1.0, sel)
        gate = jnp.where(hit, LOWEST, gate)
    return sel


def _proj_even_kernel(x_ref, g_ref, w_ref, bdq_ref, bdk_ref, qw_ref, kw_ref,
                      q_ref, k_ref, v_ref, z_ref, xbc_ref, dt_ref):
    h = _rms(x_ref[...], g_ref[...]).astype(BF16)
    y = _dot(h, w_ref[...])
    q = _headnorm_bd(y[:, :E_Q], bdq_ref[...], qw_ref[...], HD_A)
    q_ref[...] = (q * HD_A ** -0.5).astype(BF16)
    o = E_Q
    k_ref[...] = _headnorm_bd(y[:, o:o + E_KV], bdk_ref[...], kw_ref[...], HD_A)
    o += E_KV
    v_ref[...] = y[:, o:o + E_KV]
    o += E_KV
    z_ref[...] = y[:, o:o + SSM_INNER]
    o += SSM_INNER
    xbc_ref[...] = y[:, o:o + CONV_DIM]
    o += CONV_DIM
    dt_ref[...] = y[:, o:o + LANES]


def _proj_even(x, g, w, qw, kw):
    rows = x.shape[0]
    tm = ROW_TILE
    row = lambda width: pl.BlockSpec((tm, width), lambda i: (i, 0))
    full = lambda a: pl.BlockSpec(a.shape, lambda i: (0,) * a.ndim)
    bdq = _block_diag_ones(E_Q, HD_A)
    bdk = _block_diag_ones(E_KV, HD_A)
    ins = (x, g, w, bdq, bdk, qw, kw)
    widths = (E_Q, E_KV, E_KV, SSM_INNER, CONV_DIM, LANES)
    dts = (BF16, F32, F32, F32, F32, F32)
    return pl.pallas_call(
        _proj_even_kernel,
        grid=(rows // tm,),
        in_specs=[row(D_MODEL)] + [full(a) for a in ins[1:]],
        out_specs=[row(wd) for wd in widths],
        out_shape=[jax.ShapeDtypeStruct((rows, wd), dt) for wd, dt in zip(widths, dts)],
        compiler_params=pltpu.CompilerParams(dimension_semantics=("arbitrary",), vmem_limit_bytes=VMEM_LIMIT),
        name="proj_even",
    )(*ins)


def _proj_odd_kernel(seq_blocks, x_ref, g_ref, w_ref, qw_ref, kw_ref,
                     q_ref, k_ref, v_ref, kaug_ref, vaug_ref, km_ref):
    i = pl.program_id(0)
    tm = x_ref.shape[0]
    h = _rms(x_ref[...], g_ref[...]).astype(BF16)
    y = _dot(h, w_ref[...])
    qw = qw_ref[...]
    for hh in range(ATT_HEADS):
        sl = slice(hh * HD_C, (hh + 1) * HD_C)
        q_ref[:, sl] = (_rms(y[:, sl], qw) * HD_C ** -0.5).astype(BF16)
    nblk = tm // MOBA_BLOCK
    row = lax.broadcasted_iota(jnp.int32, (tm, LANES), 0)
    lane = lax.broadcasted_iota(jnp.int32, (tm, LANES), 1)
    blk = (i * nblk) % seq_blocks + sum(jnp.where(row >= b * MOBA_BLOCK, 1, 0) for b in range(1, nblk))
    onehot = jnp.where(lane == blk, 1.0, 0.0).astype(BF16)
    ones = jnp.ones((tm, LANES), BF16)
    kw = kw_ref[...]
    for kv in range(ATT_KV):
        sl = slice(kv * HD_C, (kv + 1) * HD_C)
        kh = _rms(y[:, C_Q + kv * HD_C:C_Q + (kv + 1) * HD_C], kw)
        vh = y[:, C_Q + C_KV + kv * HD_C:C_Q + C_KV + (kv + 1) * HD_C]
        k_ref[:, sl] = kh
        v_ref[:, sl] = vh
        a = 2 * kv * HD_C
        kaug_ref[:, a:a + HD_C] = kh.astype(BF16)
        kaug_ref[:, a + HD_C:a + 2 * HD_C] = onehot
        vaug_ref[:, a:a + HD_C] = vh.astype(BF16)
        vaug_ref[:, a + HD_C:a + 2 * HD_C] = ones
        for b in range(nblk):
            km_ref[b, :, sl] = jnp.mean(kh[b * MOBA_BLOCK:(b + 1) * MOBA_BLOCK], axis=0, keepdims=True)


def _proj_odd(x, g, w, qw, kw, seq_blocks):
    rows = x.shape[0]
    tm = ROW_TILE
    nblk = tm // MOBA_BLOCK
    row = lambda width: pl.BlockSpec((tm, width), lambda i: (i, 0))
    full = lambda a: pl.BlockSpec(a.shape, lambda i: (0,) * a.ndim)
    ins = (x, g, w, qw, kw)
    widths = (C_Q, C_KV, C_KV, 2 * C_KV, 2 * C_KV)
    dts = (BF16, F32, F32, BF16, BF16)
    return pl.pallas_call(
        functools.partial(_proj_odd_kernel, seq_blocks),
        grid=(rows // tm,),
        in_specs=[row(D_MODEL)] + [full(a) for a in ins[1:]],
        out_specs=[row(wd) for wd in widths] + [pl.BlockSpec((nblk, 1, C_KV), lambda i: (i, 0, 0))],
        out_shape=[jax.ShapeDtypeStruct((rows, wd), dt) for wd, dt in zip(widths, dts)]
        + [jax.ShapeDtypeStruct((rows // MOBA_BLOCK, 1, C_KV), F32)],
        compiler_params=pltpu.CompilerParams(dimension_semantics=("arbitrary",), vmem_limit_bytes=VMEM_LIMIT),
        name="proj_odd",
    )(*ins)


def _outproj_kernel(n_in, x_ref, *refs):
    o_ref = refs[-1]
    acc = x_ref[...]
    for a_ref, w_ref in zip(refs[:n_in], refs[n_in:2 * n_in]):
        acc = acc + _dot(a_ref[...], w_ref[...])
    o_ref[...] = acc


def _outproj(x, acts, ws):
    rows = x.shape[0]
    tm = ROW_TILE
    row = lambda width: pl.BlockSpec((tm, width), lambda i: (i, 0))
    full = lambda a: pl.BlockSpec(a.shape, lambda i: (0,) * a.ndim)
    return pl.pallas_call(
        functools.partial(_outproj_kernel, len(acts)),
        grid=(rows // tm,),
        in_specs=[row(D_MODEL)] + [row(a.shape[1]) for a in acts] + [full(w) for w in ws],
        out_specs=row(D_MODEL),
        out_shape=jax.ShapeDtypeStruct((rows, D_MODEL), F32),
        compiler_params=pltpu.CompilerParams(dimension_semantics=("arbitrary",), vmem_limit_bytes=VMEM_LIMIT),
        name="outproj",
    )(x, *acts, *ws)


def _ffn_kernel(x_ref, g_ref, wg_ref, wu_ref, wd_ref, o_ref, hn_ref, acc_ref):
    j = pl.program_id(1)

    @pl.when(j == 0)
    def _():
        x = x_ref[...]
        hn_ref[...] = _rms(x, g_ref[...]).astype(BF16)
        acc_ref[...] = x

    h = hn_ref[...]
    a = (_silu(_dot(h, wg_ref[...])) * _dot(h, wu_ref[...])).astype(BF16)
    acc_ref[...] += _dot(a, wd_ref[...])

    @pl.when(j == pl.num_programs(1) - 1)
    def _():
        o_ref[...] = acc_ref[...]


def _ffn(x, g, wg, wu, wd):
    rows = x.shape[0]
    d_ff = wg.shape[1]
    tm = FFN_ROW_TILE if rows % FFN_ROW_TILE == 0 else ROW_TILE
    tf = FFN_COL_TILE
    return pl.pallas_call(
        _ffn_kernel,
        grid=(rows // tm, d_ff // tf),
        in_specs=[
            pl.BlockSpec((tm, D_MODEL), lambda i, j: (i, 0)),
            pl.BlockSpec((1, D_MODEL), lambda i, j: (0, 0)),
            pl.BlockSpec((D_MODEL, tf), lambda i, j: (0, j)),
            pl.BlockSpec((D_MODEL, tf), lambda i, j: (0, j)),
            pl.BlockSpec((tf, D_MODEL), lambda i, j: (j, 0)),
        ],
        out_specs=pl.BlockSpec((tm, D_MODEL), lambda i, j: (i, 0)),
        out_shape=jax.ShapeDtypeStruct((rows, D_MODEL), F32),
        scratch_shapes=[pltpu.VMEM((tm, D_MODEL), BF16), pltpu.VMEM((tm, D_MODEL), F32)],
        compiler_params=pltpu.CompilerParams(dimension_semantics=("arbitrary", "arbitrary"), vmem_limit_bytes=VMEM_LIMIT),
        name="ffn",
    )(x, g, wg, wu, wd)


def _memkv_kernel(m_ref, wk_ref, wv_ref, bd_ref, kw_ref, k_ref, v_ref):
    mb = m_ref[...].astype(BF16)
    k_ref[...] = _headnorm_bd(_dot(mb, wk_ref[...]), bd_ref[...], kw_ref[...], MEM_HD)
    v_ref[...] = _dot(mb, wv_ref[...])


def _memkv(mem, wk, wv, kw):
    rows = mem.shape[0]
    bd = _block_diag_ones(MEM_W, MEM_HD)
    return pl.pallas_call(
        _memkv_kernel,
        out_shape=[jax.ShapeDtypeStruct((rows, MEM_W), F32)] * 2,
        compiler_params=pltpu.CompilerParams(vmem_limit_bytes=VMEM_LIMIT),
        name="memkv",
    )(mem, wk, wv, bd, kw)


def _memattn_kernel(n_seq, n, x_ref, g_ref, wq_ref, wo_ref, bd_ref, qw_ref, mk_ref, mv_ref, o_ref, q_sc, a_sc):
    x = x_ref[...]
    h = _rms(x, g_ref[...]).astype(BF16)
    q = _headnorm_bd(_dot(h, wq_ref[...]), bd_ref[...], qw_ref[...], MEM_HD)
    q_sc[...] = q * MEM_HD ** -0.5

    def one_seq(s, carry):
        r0 = pl.multiple_of(s * n, n)
        qs = q_sc[pl.ds(r0, n), :].astype(BF16)
        mk = mk_ref[s].astype(BF16)
        mv = mv_ref[s].astype(BF16)
        outs = []
        for hh in range(MEM_HEADS):
            sl = slice(hh * MEM_HD, (hh + 1) * MEM_HD)
            sc = _dot_nt(qs[:, sl], mk[:, sl])
            m = jnp.max(sc, axis=-1, keepdims=True)
            p = jnp.exp(sc - m)
            p = p / jnp.sum(p, axis=-1, keepdims=True)
            outs.append(_dot(p.astype(BF16), mv[:, sl]))
        a_sc[pl.ds(r0, n), :] = jnp.concatenate(outs, axis=1)
        return carry

    lax.fori_loop(0, n_seq, one_seq, 0)
    o_ref[...] = x + _dot(a_sc[...].astype(BF16), wo_ref[...])


def _memattn(x, g, wq, wo, qw, mk, mv, row0, n_mem, mem0, n_seq, n, steps_per_mem):
    tm = n_seq * n
    steps = n_mem // n_seq * steps_per_mem
    blk0 = row0 // tm
    memblk0 = mem0 // n_seq
    bd = _block_diag_ones(MEM_W, MEM_HD)
    full = lambda a: pl.BlockSpec(a.shape, lambda i: (0,) * a.ndim)
    row_spec = pl.BlockSpec((tm, D_MODEL), lambda i: (blk0 + i, 0))
    mem_spec = pl.BlockSpec((n_seq, MEM_LEN, MEM_W), lambda i: (memblk0 + i // steps_per_mem, 0, 0))
    return pl.pallas_call(
        functools.partial(_memattn_kernel, n_seq, n),
        grid=(steps,),
        in_specs=[row_spec, full(g), full(wq), full(wo), full(bd), full(qw), mem_spec, mem_spec],
        out_specs=row_spec,
        out_shape=jax.ShapeDtypeStruct(x.shape, F32),
        input_output_aliases={0: 0},
        scratch_shapes=[pltpu.VMEM((tm, MEM_W), F32), pltpu.VMEM((tm, MEM_W), F32)],
        compiler_params=pltpu.CompilerParams(dimension_semantics=("arbitrary",), vmem_limit_bytes=VMEM_LIMIT),
        name="memattn",
    )(x, g, wq, wo, bd, qw, mk, mv)


def _sink_softmax(s, sink):
    m = jnp.maximum(jnp.max(s, axis=-1, keepdims=True), sink)
    p = jnp.exp(s - m)
    return p / (jnp.sum(p, axis=-1, keepdims=True) + jnp.exp(sink - m))


def _swa_prompt_kernel(sink_ref, q_ref, kc_ref, kp_ref, vc_ref, vp_ref, bias_ref, o_ref):
    i = pl.program_id(1)
    tq = q_ref.shape[0]
    col = lax.broadcasted_iota(jnp.int32, (tq, 2 * tq), 1)
    first_valid = jnp.where(i > 0, 0, tq)
    colmask = col >= first_valid
    q = q_ref[...]
    outs = []
    for kv in range(ATT_KV):
        sl = slice(kv * HD_A, (kv + 1) * HD_A)
        kb = jnp.concatenate([kp_ref[:, sl], kc_ref[:, sl]], axis=0).astype(BF16)
        vb = jnp.concatenate([vp_ref[:, sl], vc_ref[:, sl]], axis=0).astype(BF16)
        for gg in range(ATT_GROUP):
            hh = kv * ATT_GROUP + gg
            s = _dot_nt(q[:, hh * HD_A:(hh + 1) * HD_A], kb) + bias_ref[hh]
            s = jnp.where(colmask, s, NEG)
            outs.append(_dot(_sink_softmax(s, sink_ref[hh]).astype(BF16), vb))
    o_ref[...] = jnp.concatenate(outs, axis=1).astype(BF16)


def _swa_prompt(sinks, q, k, v, bias, bsz, n):
    tq = WINDOW
    nq = n // tq
    cur = lambda width: pl.BlockSpec((tq, width), lambda b, i: (b * nq + i, 0))
    prev = lambda width: pl.BlockSpec((tq, width), lambda b, i: (b * nq + jnp.maximum(i - 1, 0), 0))
    return pl.pallas_call(
        _swa_prompt_kernel,
        grid=(bsz, nq),
        in_specs=[pl.BlockSpec(memory_space=pltpu.SMEM), cur(E_Q), cur(E_KV), prev(E_KV), cur(E_KV), prev(E_KV),
                  pl.BlockSpec(bias.shape, lambda b, i: (0, 0, 0))],
        out_specs=cur(E_Q),
        out_shape=jax.ShapeDtypeStruct((bsz * n, E_Q), BF16),
        compiler_params=pltpu.CompilerParams(dimension_semantics=("arbitrary", "arbitrary"), vmem_limit_bytes=VMEM_LIMIT),
        name="swa_prompt",
    )(sinks, q, k, k, v, v, bias)


def _swa_sample_kernel(n_seq, q_ref, wk_ref, wv_ref, kn_ref, vn_ref, bias_ref, sink_ref, o_ref):
    n_new = kn_ref.shape[1]
    pad = jnp.zeros((WINDOW - n_new, HD_A), F32)

    def one_seq(s, carry):
        wk, wv, kn, vn = wk_ref[s], wv_ref[s], kn_ref[s], vn_ref[s]
        for kv in range(ATT_KV):
            sl = slice(kv * HD_A, (kv + 1) * HD_A)
            kall = jnp.concatenate([wk[:, sl], kn[:, sl], pad], axis=0).astype(BF16)
            vall = jnp.concatenate([wv[:, sl], vn[:, sl], pad], axis=0).astype(BF16)
            sc = _dot_nt(q_ref[s, kv], kall) + bias_ref[kv]
            o_ref[s, kv] = _dot(_sink_softmax(sc, sink_ref[kv]).astype(BF16), vall)
        return carry

    lax.fori_loop(0, n_seq, one_seq, 0)


def _swa_sample(q, wk, wv, kn, vn, bias, sink_col, n_seq):
    sb = q.shape[0]
    lead = lambda a: pl.BlockSpec((n_seq,) + a.shape[1:], lambda i: (i,) + (0,) * (a.ndim - 1))
    full = lambda a: pl.BlockSpec(a.shape, lambda i: (0,) * a.ndim)
    return pl.pallas_call(
        functools.partial(_swa_sample_kernel, n_seq),
        grid=(sb // n_seq,),
        in_specs=[lead(q), lead(wk), lead(wv), lead(kn), lead(vn), full(bias), full(sink_col)],
        out_specs=pl.BlockSpec((n_seq,) + q.shape[1:], lambda i: (i, 0, 0, 0)),
        out_shape=jax.ShapeDtypeStruct(q.shape, F32),
        compiler_params=pltpu.CompilerParams(dimension_semantics=("arbitrary",), vmem_limit_bytes=VMEM_LIMIT),
        name="swa_sample",
    )(q, wk, wv, kn, vn, bias, sink_col)


def _ssd_chunk(xbc, prev, z, dtr, dt_t, hst, cw, cb, dtb, dtb_t, alog, alog_t, dsk, gnw):
    c = xbc.shape[0]
    row = lax.broadcasted_iota(jnp.int32, (c, CONV_DIM), 0)
    conv = cb + xbc * cw[CONV_W - 1:CONV_W]
    for s in range(1, CONV_W):
        shifted = jnp.where(row >= s, pltpu.roll(xbc, s, 0), pltpu.roll(prev, s, 0))
        conv = conv + shifted * cw[CONV_W - 1 - s:CONV_W - s]
    xc = _silu(conv)
    gw = SSM_GROUPS * SSM_N
    xs, bm, cm = xc[:, :SSM_INNER], xc[:, SSM_INNER:SSM_INNER + gw], xc[:, SSM_INNER + gw:]

    lane = lax.broadcasted_iota(jnp.int32, (1, LANES), 1)
    a_row = jnp.where(lane < SSM_HEADS, -jnp.exp(alog), 0.0)
    dt = _softplus(dtr + dtb)
    dta = dt * a_row
    dta_t = _softplus(dt_t + dtb_t) * (-jnp.exp(alog_t))
    ti = lax.broadcasted_iota(jnp.int32, (c, c), 0)
    si = lax.broadcasted_iota(jnp.int32, (c, c), 1)
    tri = ti >= si
    lower = jnp.where(tri, 1.0, 0.0).astype(BF16)
    upper = jnp.where(ti <= si, 1.0, 0.0).astype(BF16)
    cs_col = sum(_dot(lower, part) for part in _split3(dta))
    cs_row = sum(_dot(part, upper) for part in _split3(dta_t))
    ecs = jnp.exp(cs_col)
    cs_last = cs_col[c - 1:c, :]
    tail = jnp.exp(cs_last - cs_col)
    elast = jnp.exp(cs_last)

    ys, hs = [], []
    hpg = SSM_HEADS // SSM_GROUPS
    for g in range(SSM_GROUPS):
        bg = bm[:, g * SSM_N:(g + 1) * SSM_N]
        cgb = cm[:, g * SSM_N:(g + 1) * SSM_N].astype(BF16)
        gmat = _dot_nt(cgb, bg.astype(BF16))
        for hh in range(g * hpg, (g + 1) * hpg):
            seg = cs_col[:, hh:hh + 1] - cs_row[hh:hh + 1, :]
            decay = jnp.exp(jnp.where(tri, seg, NEG))
            xdt = (xs[:, hh * SSM_P:(hh + 1) * SSM_P] * dt[:, hh:hh + 1]).astype(BF16)
            h_h = hst[hh * SSM_P:(hh + 1) * SSM_P, :]
            ys.append(_dot((gmat * decay).astype(BF16), xdt) + _dot_nt(cgb, h_h.astype(BF16)) * ecs[:, hh:hh + 1])
            hs.append(h_h * elast[:, hh:hh + 1] + _dot_tn(xdt, (bg * tail[:, hh:hh + 1]).astype(BF16)))
    y = jnp.concatenate(ys, axis=1) + xs * dsk
    y = y * _silu(z)
    gn = SSM_INNER // SSM_GROUPS
    y = jnp.concatenate([_rms(y[:, g * gn:(g + 1) * gn], gnw[:, g * gn:(g + 1) * gn]) for g in range(SSM_GROUPS)], axis=1)
    return y, jnp.concatenate(hs, axis=0)


def _ssd_prompt_kernel(xbc_ref, prev_ref, z_ref, dt_ref, dtt_ref, cw_ref, cb_ref, dtb_ref, dtbt_ref, alog_ref, alogt_ref,
                       dsk_ref, gnw_ref, y_ref, hl_ref, hs_ref):
    c = pl.program_id(1)

    @pl.when(c == 0)
    def _():
        hs_ref[...] = jnp.zeros(hs_ref.shape, F32)

    prev = prev_ref[...] * (c > 0).astype(F32)
    y, hn = _ssd_chunk(xbc_ref[...], prev, z_ref[...], dt_ref[...], dtt_ref[0], hs_ref[...], cw_ref[...], cb_ref[...],
                       dtb_ref[...], dtbt_ref[...], alog_ref[...], alogt_ref[...], dsk_ref[...], gnw_ref[...])
    y_ref[...] = y.astype(BF16)
    hs_ref[...] = hn
    hl_ref[0] = hn


def _ssd_prompt(xbc, z, dt, dt_t, params, bsz, n):
    c = SSD_CHUNK
    nc = n // c
    cur = lambda width: pl.BlockSpec((c, width), lambda b, i: (b * nc + i, 0))
    full = lambda a: pl.BlockSpec(a.shape, lambda b, i: (0,) * a.ndim)
    hd = SSM_HEADS * SSM_P
    return pl.pallas_call(
        _ssd_prompt_kernel,
        grid=(bsz, nc),
        in_specs=[cur(CONV_DIM), pl.BlockSpec((c, CONV_DIM), lambda b, i: (b * nc + jnp.maximum(i - 1, 0), 0)),
                  cur(SSM_INNER), cur(LANES), pl.BlockSpec((1, SSM_HEADS, c), lambda b, i: (b, 0, i))]
        + [full(p) for p in params],
        out_specs=[cur(SSM_INNER), pl.BlockSpec((1, hd, SSM_N), lambda b, i: (b, 0, 0))],
        out_shape=[jax.ShapeDtypeStruct((bsz * n, SSM_INNER), BF16), jax.ShapeDtypeStruct((bsz, hd, SSM_N), F32)],
        scratch_shapes=[pltpu.VMEM((hd, SSM_N), F32)],
        compiler_params=pltpu.CompilerParams(dimension_semantics=("arbitrary", "arbitrary"), vmem_limit_bytes=VMEM_LIMIT),
        name="ssd_prompt",
    )(xbc, xbc, z, dt, dt_t, *params)


def _ssd_sample_kernel(n_seq, xbc_ref, st_ref, z_ref, dt_ref, dtt_ref, h0_ref, cw_ref, cb_ref, dtb_ref, dtbt_ref,
                       alog_ref, alogt_ref, dsk_ref, gnw_ref, y_ref, hn_ref):
    def one_seq(s, carry):
        y, hn = _ssd_chunk(xbc_ref[s], st_ref[s], z_ref[s], dt_ref[s], dtt_ref[s], h0_ref[s], cw_ref[...], cb_ref[...],
                           dtb_ref[...], dtbt_ref[...], alog_ref[...], alogt_ref[...], dsk_ref[...], gnw_ref[...])
        y_ref[s] = y
        hn_ref[s] = hn
        return carry

    lax.fori_loop(0, n_seq, one_seq, 0)


def _ssd_sample(xbc, st, z, dt, dt_t, h0, params, n_seq):
    sb = xbc.shape[0]
    lead = lambda a: pl.BlockSpec((n_seq,) + a.shape[1:], lambda i: (i, 0, 0))
    full = lambda a: pl.BlockSpec(a.shape, lambda i: (0,) * a.ndim)
    seq = (xbc, st, z, dt, dt_t, h0)
    return pl.pallas_call(
        functools.partial(_ssd_sample_kernel, n_seq),
        grid=(sb // n_seq,),
        in_specs=[lead(a) for a in seq] + [full(p) for p in params],
        out_specs=[lead(z), lead(h0)],
        out_shape=[jax.ShapeDtypeStruct(z.shape, F32), jax.ShapeDtypeStruct(h0.shape, F32)],
        compiler_params=pltpu.CompilerParams(dimension_semantics=("arbitrary",), vmem_limit_bytes=VMEM_LIMIT),
        name="ssd_sample",
    )(*seq, *params)


def _moba_prompt_kernel(q_ref, k_ref, v_ref, km_ref, bown_ref, bprev_ref, cfar_ref, o_ref, qa_ref, m_ref, acc_ref):
    cur = pl.program_id(2)
    tq = q_ref.shape[0]
    rows = ATT_GROUP * tq
    qs = jnp.concatenate([q_ref[:, g * HD_C:(g + 1) * HD_C] for g in range(ATT_GROUP)], axis=0)
    lane = lax.broadcasted_iota(jnp.int32, (rows, LANES), 1)
    gate = jnp.where(lane < cur, _dot_nt(qs, km_ref[0, 0]), NEG)
    sel = _top3_mask(gate, lane.astype(F32))
    allowed = jnp.where(lane < cur, sel, jnp.where(lane == cur, 1.0, 0.0))
    qa_ref[:, :HD_C] = qs
    qa_ref[:, HD_C:] = jnp.where(allowed > 0.5, 0.0, NEG).astype(BF16)
    m_ref[...] = jnp.full(m_ref.shape, LOWEST, F32)
    acc_ref[...] = jnp.zeros(acc_ref.shape, F32)

    def attend(j, bias):
        r0 = pl.multiple_of(j * MOBA_BLOCK, MOBA_BLOCK)
        s = _dot_nt(qa_ref[...], k_ref[pl.ds(r0, MOBA_BLOCK), :]) + bias
        m_old = m_ref[...]
        m_new = jnp.maximum(m_old, jnp.max(s, axis=-1, keepdims=True))
        p = jnp.exp(s - m_new).astype(BF16)
        acc_ref[...] = jnp.exp(m_old - m_new) * acc_ref[...] + _dot(p, v_ref[pl.ds(r0, MOBA_BLOCK), :])
        m_ref[...] = m_new

    attend(cur, bown_ref[0])

    @pl.when(cur >= 1)
    def _():
        attend(cur - 1, bprev_ref[0])

    def far(j, carry):
        attend(j, cfar_ref[0])
        return carry

    lax.fori_loop(0, jnp.maximum(cur - 1, 0), far, 0)
    acc = acc_ref[...]
    o = acc[:, :HD_C] / acc[:, HD_C:]
    for g in range(ATT_GROUP):
        o_ref[:, g * HD_C:(g + 1) * HD_C] = o[g * tq:(g + 1) * tq].astype(BF16)


def _moba_prompt(q, kaug, vaug, kmean, bown, bprev, cfar, bsz, n):
    tq = MOBA_BLOCK
    nq = n // tq
    rows = ATT_GROUP * tq
    gw = ATT_GROUP * HD_C
    qspec = pl.BlockSpec((tq, gw), lambda b, kv, i: (b * nq + i, kv))
    kvspec = pl.BlockSpec((n, 2 * HD_C), lambda b, kv, i: (b, kv))
    tab = lambda a: pl.BlockSpec((1,) + a.shape[1:], lambda b, kv, i: (kv, 0, 0))
    return pl.pallas_call(
        _moba_prompt_kernel,
        grid=(bsz, ATT_KV, nq),
        in_specs=[qspec, kvspec, kvspec, pl.BlockSpec((1, 1, LANES, HD_C), lambda b, kv, i: (b, kv, 0, 0)),
                  tab(bown), tab(bprev), tab(cfar)],
        out_specs=qspec,
        out_shape=jax.ShapeDtypeStruct((bsz * n, C_Q), BF16),
        scratch_shapes=[pltpu.VMEM((rows, 2 * HD_C), BF16), pltpu.VMEM((rows, 1), F32), pltpu.VMEM((rows, 2 * HD_C), F32)],
        compiler_params=pltpu.CompilerParams(dimension_semantics=("arbitrary",) * 3, vmem_limit_bytes=VMEM_LIMIT),
        name="moba_prompt",
    )(q, kaug, vaug, kmean, bown, bprev, cfar)


def _moba_sample_kernel(ppb, pt_ref, q_ref, *refs):
    k_pages, v_pages = refs[:ppb], refs[ppb:2 * ppb]
    kn_ref, vn_ref, blast_ref, cfar_ref, bown_ref, o_ref, gate_sc, m_sc, l_sc, o_sc = refs[2 * ppb:]
    j = pl.program_id(1)
    nblk = pl.num_programs(1)
    rows = q_ref.shape[2]
    lane = lax.broadcasted_iota(jnp.int32, (rows, LANES), 1)

    @pl.when(j == 0)
    def _():
        gate_sc[...] = jnp.zeros(gate_sc.shape, F32)
        m_sc[...] = jnp.zeros(m_sc.shape, F32)
        l_sc[...] = jnp.zeros(l_sc.shape, F32)

    kblk = jnp.concatenate([r[0] for r in k_pages], axis=0)
    vblk = jnp.concatenate([r[0] for r in v_pages], axis=0)
    hit = lane == j
    for kv in range(ATT_KV):
        sl = slice(kv * HD_C, (kv + 1) * HD_C)
        kk = kblk[:, sl]
        q = q_ref[0, kv]
        kmean = jnp.mean(kk, axis=0, keepdims=True).astype(BF16).astype(F32)
        gate = jnp.sum(q.astype(F32) * kmean, axis=-1, keepdims=True)
        s = _dot_nt(q, kk.astype(BF16)) + jnp.where(j == nblk - 1, blast_ref[kv], cfar_ref[kv])
        m = jnp.max(s, axis=-1, keepdims=True)
        p = jnp.exp(s - m)
        gate_sc[kv] = jnp.where(hit, gate, gate_sc[kv])
        m_sc[kv] = jnp.where(hit, m, m_sc[kv])
        l_sc[kv] = jnp.where(hit, jnp.sum(p, axis=-1, keepdims=True), l_sc[kv])
        o_sc[kv, j] = _dot(p.astype(BF16), vblk[:, sl].astype(BF16))

    @pl.when(j == nblk - 1)
    def _():
        n_new = kn_ref.shape[1]
        pad = jnp.zeros((LANES - n_new, HD_C), F32)
        for kv in range(ATT_KV):
            sl = slice(kv * HD_C, (kv + 1) * HD_C)
            q = q_ref[0, kv]
            sel = _top3_mask(jnp.where(lane < nblk, gate_sc[kv], LOWEST), lane.astype(F32)) > 0.5
            m_blk = jnp.where(sel, m_sc[kv], LOWEST)
            kn = jnp.concatenate([kn_ref[0][:, sl], pad], axis=0).astype(BF16)
            vn = jnp.concatenate([vn_ref[0][:, sl], pad], axis=0).astype(BF16)
            s_own = _dot_nt(q, kn) + bown_ref[kv]
            m_tot = jnp.maximum(jnp.max(m_blk, axis=-1, keepdims=True), jnp.max(s_own, axis=-1, keepdims=True))
            w = jnp.exp(m_blk - m_tot)
            p_own = jnp.exp(s_own - m_tot)
            den = jnp.sum(w * l_sc[kv], axis=-1, keepdims=True) + jnp.sum(p_own, axis=-1, keepdims=True)
            o = _dot(p_own.astype(BF16), vn)
            for b in range(o_sc.shape[1]):
                o = o + w[:, b:b + 1] * o_sc[kv, b]
            o_ref[0, kv] = o / den


def _moba_sample(pt, q, pool_k, pool_v, kn, vn, blast, cfar, bown, n_pages):
    sb, _, rows, _ = q.shape
    ppb = MOBA_BLOCK // PAGE_SIZE
    nblk = n_pages // ppb

    def page(p):
        return pl.BlockSpec((1, PAGE_SIZE, C_KV), lambda b, j, pt_ref: (pt_ref[b * n_pages + j * ppb + p], 0, 0))

    seq = lambda a: pl.BlockSpec((1,) + a.shape[1:], lambda b, j, pt_ref: (b,) + (0,) * (a.ndim - 1))
    full = lambda a: pl.BlockSpec(a.shape, lambda b, j, pt_ref: (0,) * a.ndim)
    grid_spec = pltpu.PrefetchScalarGridSpec(
        num_scalar_prefetch=1,
        grid=(sb, nblk),
        in_specs=[seq(q)] + [page(p) for p in range(ppb)] * 2 + [seq(kn), seq(vn), full(blast), full(cfar), full(bown)],
        out_specs=seq(q),
        scratch_shapes=[pltpu.VMEM((ATT_KV, rows, LANES), F32)] * 3 + [pltpu.VMEM((ATT_KV, nblk, rows, HD_C), F32)],
    )
    return pl.pallas_call(
        functools.partial(_moba_sample_kernel, ppb),
        grid_spec=grid_spec,
        out_shape=jax.ShapeDtypeStruct(q.shape, F32),
        compiler_params=pltpu.CompilerParams(dimension_semantics=("arbitrary", "arbitrary"), vmem_limit_bytes=VMEM_LIMIT),
        name="moba_sample",
    )(pt, q, *([pool_k] * ppb), *([pool_v] * ppb), kn, vn, blast, cfar, bown)


def _bias_table(tab_t, dist, valid):
    idx = _bucket_np(dist).reshape(-1)
    t = jnp.take(tab_t, jnp.asarray(idx), axis=1).reshape(tab_t.shape[0], *dist.shape)
    return jnp.where(jnp.asarray(valid)[None], t, NEG)


def _group_rows(t):
    return t.reshape(ATT_KV, ATT_GROUP * t.shape[1], t.shape[2])


def _to_group_rows(a, sb, sn, hd):
    return a.reshape(sb, sn, ATT_KV, ATT_GROUP, hd).transpose(0, 2, 3, 1, 4).reshape(sb, ATT_KV, ATT_GROUP * sn, hd)


def _from_group_rows(a, sb, sn, hd):
    return a.reshape(sb, ATT_KV, ATT_GROUP, sn, hd).transpose(0, 3, 1, 2, 4).reshape(sb * sn, ATT_HEADS * hd)


def _pad_lanes(v):
    return jnp.pad(v.astype(F32), (0, LANES - v.shape[0]))[None]


def kernel(x_prompt, x_sample, cache_win_k, cache_win_v, state_conv, state_ssm, cache_moba_k, cache_moba_v, cache_mem_k, cache_mem_v, page_table, mem_prompt, rel_bias, norm_mix, norm_mem, norm_ffn, even_w_in, even_w_out, even_q_norm, even_k_norm, even_sinks, even_conv_w, even_conv_b, even_dt_bias, even_a_log, even_d_skip, even_gate_norm, odd_w_in, odd_w_out, odd_q_norm, odd_k_norm, mem_wq, mem_wk, mem_wv, mem_wo, mem_q_norm, mem_k_norm, ffn_w_gate, ffn_w_up, ffn_w_down):
    bsz, n = x_prompt.shape[:2]
    sb, sn = x_sample.shape[:2]
    depth = norm_mix.shape[0]
    rp, rs = bsz * n, sb * sn
    n_pages = page_table.shape[1]
    n_pool = cache_moba_k.shape[1]
    past_len = n_pages * PAGE_SIZE
    assert past_len % MOBA_BLOCK == 0 and cache_win_k.shape[2] == WINDOW and sn >= CONV_W - 1
    seq_group = 16

    x = jnp.concatenate([x_prompt.reshape(rp, D_MODEL), x_sample.reshape(rs, D_MODEL)], axis=0)
    tab_t = rel_bias.T.astype(F32)

    qi = np.arange(WINDOW)[:, None]
    kj = np.arange(2 * WINDOW)[None, :]
    d = WINDOW + qi - kj
    swa_bias_p = _bias_table(tab_t, d, (d >= 0) & (d < WINDOW))
    ti = np.arange(sn)[:, None]
    d = np.where(kj < WINDOW, WINDOW + ti - kj, ti - (kj - WINDOW))
    swa_bias_s = _group_rows(_bias_table(tab_t, d, (d >= 0) & (d < WINDOW) & (kj < WINDOW + sn)))
    qi = np.arange(MOBA_BLOCK)[:, None]
    kj = np.arange(MOBA_BLOCK)[None, :]
    bown_p = _group_rows(_bias_table(tab_t, qi - kj, qi >= kj))
    bprev_p = _group_rows(_bias_table(tab_t, MOBA_BLOCK + qi - kj, np.ones((MOBA_BLOCK, MOBA_BLOCK), bool)))
    far = np.full((MOBA_BLOCK, 1), 2 * MOBA_BLOCK)
    assert _bucket_np(MOBA_BLOCK + 1) == N_BUCKETS - 1
    cfar_p = _group_rows(_bias_table(tab_t, far, np.ones_like(far, bool)))
    blast_s = _group_rows(_bias_table(tab_t, MOBA_BLOCK + ti - kj, np.ones((sn, MOBA_BLOCK), bool)))
    cfar_s = _group_rows(_bias_table(tab_t, far[:sn], np.ones((sn, 1), bool)))
    kl = np.arange(LANES)[None, :]
    bown_s = _group_rows(_bias_table(tab_t, ti - kl, (ti >= kl) & (kl < sn)))

    pool_k = cache_moba_k.reshape(-1, PAGE_SIZE, C_KV)
    pool_v = cache_moba_v.reshape(-1, PAGE_SIZE, C_KV)
    pt_flat = page_table.reshape(-1)
    mem_k = cache_mem_k.reshape(depth * sb, MEM_LEN, MEM_W)
    mem_v = cache_mem_v.reshape(depth * sb, MEM_LEN, MEM_W)
    mem_rows = mem_prompt.reshape(bsz * MEM_LEN, D_MODEL)

    pwk, pwv, pcv, pss, pmk, pmv, pxk, pxv = [], [], [], [], [], [], [], []
    swk, swv, scv, sss, smk, smv = [], [], [], [], [], []
    for l in range(depth):
        g_mix = norm_mix[l][None]
        if l % 2 == 0:
            e = l // 2
            w_in = jnp.pad(even_w_in[e], ((0, 0), (0, EVEN_W - even_w_in.shape[2]))).astype(BF16)
            w_out = even_w_out[e].astype(BF16)
            qw = jnp.tile(even_q_norm[e], ATT_HEADS)[None]
            kw = jnp.tile(even_k_norm[e], ATT_KV)[None]
            q, k, v, z, xbc, dt = _proj_even(x, g_mix, w_in, qw, kw)
            params = (even_conv_w[e], even_conv_b[e][None], _pad_lanes(even_dt_bias[e]), even_dt_bias[e][:, None],
                      _pad_lanes(even_a_log[e]), even_a_log[e][:, None], jnp.repeat(even_d_skip[e], SSM_P)[None],
                      even_gate_norm[e][None])
            att_p = _swa_prompt(even_sinks[e], q, k, v, swa_bias_p, bsz, n)
            dt_t = dt[:rp, :SSM_HEADS].reshape(bsz, n, SSM_HEADS).transpose(0, 2, 1)
            y_p, h_p = _ssd_prompt(xbc, z, dt, dt_t, params, bsz, n)
            sink_col = jnp.broadcast_to(even_sinks[e].reshape(ATT_KV, ATT_GROUP, 1, 1), (ATT_KV, ATT_GROUP, sn, 1))
            sink_col = sink_col.reshape(ATT_KV, ATT_GROUP * sn, 1)
            k_s = k[rp:].reshape(sb, sn, E_KV)
            v_s = v[rp:].reshape(sb, sn, E_KV)
            att_s = _swa_sample(_to_group_rows(q[rp:], sb, sn, HD_A), cache_win_k[e].reshape(sb, WINDOW, E_KV),
                                cache_win_v[e].reshape(sb, WINDOW, E_KV), k_s, v_s, swa_bias_s, sink_col, seq_group)
            xbc_s = xbc[rp:].reshape(sb, sn, CONV_DIM)
            st = jnp.pad(state_conv[e], ((0, 0), (sn - (CONV_W - 1), 0), (0, 0)))
            dt_s = dt[rp:].reshape(sb, sn, LANES)
            y_s, h_s = _ssd_sample(xbc_s, st, z[rp:].reshape(sb, sn, SSM_INNER), dt_s,
                                   dt_s[:, :, :SSM_HEADS].transpose(0, 2, 1),
                                   state_ssm[e].reshape(sb, SSM_HEADS * SSM_P, SSM_N), params, seq_group)
            att = jnp.concatenate([att_p, _from_group_rows(att_s, sb, sn, HD_A).astype(BF16)], axis=0)
            yy = jnp.concatenate([y_p, y_s.reshape(rs, SSM_INNER).astype(BF16)], axis=0)
            x = _outproj(x, [att, yy], [w_out[:E_Q], w_out[E_Q:]])
            kp = k[:rp].reshape(bsz, n, ATT_KV, HD_A)
            vp = v[:rp].reshape(bsz, n, ATT_KV, HD_A)
            pwk.append(kp[:, -WINDOW:])
            pwv.append(vp[:, -WINDOW:])
            pcv.append(xbc[:rp].reshape(bsz, n, CONV_DIM)[:, -(CONV_W - 1):])
            pss.append(h_p.reshape(bsz, SSM_HEADS, SSM_P, SSM_N))
            swk.append(jnp.concatenate([cache_win_k[e], k_s.reshape(sb, sn, ATT_KV, HD_A)], axis=1)[:, -WINDOW:])
            swv.append(jnp.concatenate([cache_win_v[e], v_s.reshape(sb, sn, ATT_KV, HD_A)], axis=1)[:, -WINDOW:])
            scv.append(jnp.concatenate([state_conv[e], xbc_s], axis=1)[:, -(CONV_W - 1):])
            sss.append(h_s.reshape(sb, SSM_HEADS, SSM_P, SSM_N))
        else:
            o = l // 2
            w_in = odd_w_in[o].astype(BF16)
            w_out = odd_w_out[o].astype(BF16)
            nb = n // MOBA_BLOCK
            q, k, v, kaug, vaug, km = _proj_odd(x, g_mix, w_in, odd_q_norm[o][None], odd_k_norm[o][None], nb)
            kmean = km[:bsz * nb, 0].reshape(bsz, nb, ATT_KV, HD_C).transpose(0, 2, 1, 3)
            kmean = jnp.pad(kmean, ((0, 0), (0, 0), (0, LANES - nb), (0, 0))).astype(BF16)
            att_p = _moba_prompt(q, kaug, vaug, kmean, bown_p, bprev_p, cfar_p, bsz, n)
            k_s = k[rp:].reshape(sb, sn, C_KV)
            v_s = v[rp:].reshape(sb, sn, C_KV)
            att_s = _moba_sample(pt_flat + o * n_pool, _to_group_rows(q[rp:], sb, sn, HD_C), pool_k, pool_v, k_s, v_s,
                                 blast_s, cfar_s, bown_s, n_pages)
            att = jnp.concatenate([att_p, _from_group_rows(att_s, sb, sn, HD_C).astype(BF16)], axis=0)
            x = _outproj(x, [att], [w_out])
            pmk.append(k[:rp].reshape(bsz, n, ATT_KV, HD_C))
            pmv.append(v[:rp].reshape(bsz, n, ATT_KV, HD_C))
            smk.append(k_s.reshape(sb, sn, ATT_KV, HD_C))
            smv.append(v_s.reshape(sb, sn, ATT_KV, HD_C))
        mqw = jnp.tile(mem_q_norm[l], MEM_HEADS)[None]
        mkw = jnp.tile(mem_k_norm[l], MEM_HEADS)[None]
        wq = mem_wq[l].astype(BF16)
        wo = mem_wo[l].astype(BF16)
        xk, xv = _memkv(mem_rows, mem_wk[l].astype(BF16), mem_wv[l].astype(BF16), mkw)
        pxk.append(xk.reshape(bsz, MEM_LEN, MEM_HEADS, MEM_HD))
        pxv.append(xv.reshape(bsz, MEM_LEN, MEM_HEADS, MEM_HD))
        g_mem = norm_mem[l][None]
        x = _memattn(x, g_mem, wq, wo, mqw, xk.reshape(bsz, MEM_LEN, MEM_W), xv.reshape(bsz, MEM_LEN, MEM_W),
                     0, bsz, 0, 1, ROW_TILE, n // ROW_TILE)
        x = _memattn(x, g_mem, wq, wo, mqw, mem_k, mem_v, rp, sb, l * sb, seq_group, sn, 1)
        x = _ffn(x, norm_ffn[l][None], ffn_w_gate[l].astype(BF16), ffn_w_up[l].astype(BF16), ffn_w_down[l].astype(BF16))

    return (x[:rp].reshape(bsz, n, D_MODEL), x[rp:].reshape(sb, sn, D_MODEL),
            jnp.stack(pwk), jnp.stack(pwv), jnp.stack(pcv), jnp.stack(pss), jnp.stack(pmk), jnp.stack(pmv),
            jnp.stack(pxk), jnp.stack(pxv), jnp.stack(swk), jnp.stack(swv), jnp.stack(scv), jnp.stack(sss),
            jnp.stack(smk), jnp.stack(smv))
```

```python
import functools
import math

import jax
import jax.numpy as jnp
import numpy as np
from jax import lax
from jax.experimental import pallas as pl
from jax.experimental.pallas import tpu as pltpu

F32 = jnp.float32
BF16 = jnp.bfloat16

D_MODEL = 1024
ATT_HEADS = 8
ATT_KV = 2
ATT_GROUP = ATT_HEADS // ATT_KV
N_BUCKETS = 32
MAX_DIST = 128
WINDOW = 128
HD_A = 64
SSM_HEADS = 8
SSM_P = 64
SSM_INNER = SSM_HEADS * SSM_P
SSM_GROUPS = 2
SSM_N = 64
CONV_W = 4
CONV_DIM = SSM_INNER + 2 * SSM_GROUPS * SSM_N
SSD_CHUNK = 128
HD_C = 128
MOBA_BLOCK = 256
MOBA_TOPK = 3
PAGE_SIZE = 128
MEM_LEN = 256
MEM_HEADS = 4
MEM_HD = 64
MEM_W = MEM_HEADS * MEM_HD
E_Q = ATT_HEADS * HD_A
E_KV = ATT_KV * HD_A
C_Q = ATT_HEADS * HD_C
C_KV = ATT_KV * HD_C
EPS = 1e-6
NEG = -1e30
LOWEST = -3e38

LOG2E = 1.4426950408889634
LANES = 128
CONST_LANE = LANES - 2
SUB_ROWS = 128
SAMPLE_BLOCKS_PER_STEP = 4
SEQ_UNROLL = 4
EVEN_W = E_Q + 2 * E_KV + SSM_INNER + CONV_DIM + LANES
ROW_TILE = 512
FFN_ROW_TILE = 1024
FFN_COL_TILE = 256
VMEM_LIMIT = 56 * 1024 * 1024


def _dot(a, b):
    return jnp.dot(a, b, preferred_element_type=F32)


def _dot_nt(a, b):
    return lax.dot_general(a, b, (((1,), (1,)), ((), ())), preferred_element_type=F32)


def _dot_tn(a, b):
    return lax.dot_general(a, b, (((0,), (0,)), ((), ())), preferred_element_type=F32)


def _split3(x):
    hi = x.astype(BF16)
    r = x - hi.astype(F32)
    mid = r.astype(BF16)
    lo = (r - mid.astype(F32)).astype(BF16)
    return hi, mid, lo


def _rms(x, w):
    ms = jnp.mean(x * x, axis=-1, keepdims=True)
    return x * lax.rsqrt(ms + EPS) * w


def _headnorm_bd(y, ones_bd, w, hd):
    sq = y * y
    hi = sq.astype(BF16)
    lo = (sq - hi.astype(F32)).astype(BF16)
    ss = _dot(hi, ones_bd) + _dot(lo, ones_bd)
    return y * lax.rsqrt(ss * (1.0 / hd) + EPS) * w


def _silu(x):
    return x * jax.nn.sigmoid(x)


def _softplus(x):
    return jnp.maximum(x, 0.0) + jnp.log1p(jnp.exp(-jnp.abs(x)))


def _bucket_np(dist):
    n = np.maximum(np.asarray(dist, np.int64), 0)
    nf = np.maximum(n, 1).astype(np.float64)
    exact = N_BUCKETS // 2
    large = exact + np.floor(np.log(nf / exact) / math.log(MAX_DIST / exact) * (N_BUCKETS - exact) + 1e-6).astype(np.int64)
    return np.where(n < exact, n, np.minimum(large, N_BUCKETS - 1)).astype(np.int32)


def _block_diag_ones(width, hd):
    i = np.arange(width) // hd
    return jnp.asarray((i[:, None] == i[None, :]).astype(np.float32), BF16)


def _top3_mask(gate, lane_f):
    sel = jnp.zeros_like(gate)
    for _ in range(MOBA_TOPK):
        mx = jnp.max(gate, axis=-1, keepdims=True)
        idx = jnp.min(jnp.where(gate == mx, lane_f, 1e9), axis=-1, keepdims=True)
        hit = lane_f == idx
        sel = jnp.where(hit, 1.0, sel)
        gate = jnp.where(hit, LOWEST, gate)
    return sel


def _proj_even_kernel(x_ref, g_ref, w_ref, bdq_ref, bdk_ref, qw_ref, kw_ref,
                      q_ref, k_ref, v_ref, z_ref, xbc_ref, dt_ref):
    h = _rms(x_ref[...], g_ref[...]).astype(BF16)
    y = _dot(h, w_ref[...])
    q = _headnorm_bd(y[:, :E_Q], bdq_ref[...], qw_ref[...], HD_A)
    q_ref[...] = (q * HD_A ** -0.5).astype(BF16)
    o = E_Q
    k_ref[...] = _headnorm_bd(y[:, o:o + E_KV], bdk_ref[...], kw_ref[...], HD_A)
    o += E_KV
    v_ref[...] = y[:, o:o + E_KV]
    o += E_KV
    z_ref[...] = y[:, o:o + SSM_INNER]
    o += SSM_INNER
    xbc_ref[...] = y[:, o:o + CONV_DIM]
    o += CONV_DIM
    dt_ref[...] = y[:, o:o + LANES]


def _proj_even(x, g, w, qw, kw):
    rows = x.shape[0]
    tm = ROW_TILE
    row = lambda width: pl.BlockSpec((tm, width), lambda i: (i, 0))
    full = lambda a: pl.BlockSpec(a.shape, lambda i: (0,) * a.ndim)
    bdq = _block_diag_ones(E_Q, HD_A)
    bdk = _block_diag_ones(E_KV, HD_A)
    ins = (x, g, w, bdq, bdk, qw, kw)
    widths = (E_Q, E_KV, E_KV, SSM_INNER, CONV_DIM, LANES)
    dts = (BF16, F32, F32, F32, F32, F32)
    return pl.pallas_call(
        _proj_even_kernel,
        grid=(rows // tm,),
        in_specs=[row(D_MODEL)] + [full(a) for a in ins[1:]],
        out_specs=[row(wd) for wd in widths],
        out_shape=[jax.ShapeDtypeStruct((rows, wd), dt) for wd, dt in zip(widths, dts)],
        compiler_params=pltpu.CompilerParams(dimension_semantics=("arbitrary",), vmem_limit_bytes=VMEM_LIMIT),
        name="proj_even",
    )(*ins)


def _proj_odd_kernel(seq_blocks, x_ref, g_ref, w_ref, qw_ref, kw_ref,
                     q_ref, k_ref, v_ref, kaug_ref, vaug_ref, km_ref):
    i = pl.program_id(0)
    tm = x_ref.shape[0]
    h = _rms(x_ref[...], g_ref[...]).astype(BF16)
    y = _dot(h, w_ref[...])
    qw = qw_ref[...]
    for hh in range(ATT_HEADS):
        sl = slice(hh * HD_C, (hh + 1) * HD_C)
        q_ref[:, sl] = (_rms(y[:, sl], qw) * (HD_C ** -0.5 * LOG2E)).astype(BF16)
    nblk = tm // MOBA_BLOCK
    row = lax.broadcasted_iota(jnp.int32, (tm, LANES), 0)
    lane = lax.broadcasted_iota(jnp.int32, (tm, LANES), 1)
    blk = (i * nblk) % seq_blocks + sum(jnp.where(row >= b * MOBA_BLOCK, 1, 0) for b in range(1, nblk))
    onehot = jnp.where(jnp.logical_or(lane == blk, lane >= CONST_LANE), 1.0, 0.0).astype(BF16)
    ones = jnp.ones((tm, LANES), BF16)
    kw = kw_ref[...]
    for kv in range(ATT_KV):
        sl = slice(kv * HD_C, (kv + 1) * HD_C)
        kh = _rms(y[:, C_Q + kv * HD_C:C_Q + (kv + 1) * HD_C], kw)
        vh = y[:, C_Q + C_KV + kv * HD_C:C_Q + C_KV + (kv + 1) * HD_C]
        k_ref[:, sl] = kh
        v_ref[:, sl] = vh
        a = 2 * kv * HD_C
        kaug_ref[:, a:a + HD_C] = kh.astype(BF16)
        kaug_ref[:, a + HD_C:a + 2 * HD_C] = onehot
        vaug_ref[:, a:a + HD_C] = vh.astype(BF16)
        vaug_ref[:, a + HD_C:a + 2 * HD_C] = ones
        for b in range(nblk):
            km_ref[b, :, sl] = jnp.mean(kh[b * MOBA_BLOCK:(b + 1) * MOBA_BLOCK], axis=0, keepdims=True)


def _proj_odd(x, g, w, qw, kw, seq_blocks):
    rows = x.shape[0]
    tm = ROW_TILE
    nblk = tm // MOBA_BLOCK
    row = lambda width: pl.BlockSpec((tm, width), lambda i: (i, 0))
    full = lambda a: pl.BlockSpec(a.shape, lambda i: (0,) * a.ndim)
    ins = (x, g, w, qw, kw)
    widths = (C_Q, C_KV, C_KV, 2 * C_KV, 2 * C_KV)
    dts = (BF16, F32, F32, BF16, BF16)
    return pl.pallas_call(
        functools.partial(_proj_odd_kernel, seq_blocks),
        grid=(rows // tm,),
        in_specs=[row(D_MODEL)] + [full(a) for a in ins[1:]],
        out_specs=[row(wd) for wd in widths] + [pl.BlockSpec((nblk, 1, C_KV), lambda i: (i, 0, 0))],
        out_shape=[jax.ShapeDtypeStruct((rows, wd), dt) for wd, dt in zip(widths, dts)]
        + [jax.ShapeDtypeStruct((rows // MOBA_BLOCK, 1, C_KV), F32)],
        compiler_params=pltpu.CompilerParams(dimension_semantics=("arbitrary",), vmem_limit_bytes=VMEM_LIMIT),
        name="proj_odd",
    )(*ins)


def _outproj_kernel(n_in, x_ref, *refs):
    o_ref = refs[-1]
    acc = x_ref[...]
    for a_ref, w_ref in zip(refs[:n_in], refs[n_in:2 * n_in]):
        acc = acc + _dot(a_ref[...], w_ref[...])
    o_ref[...] = acc


def _outproj(x, acts, ws):
    rows = x.shape[0]
    tm = ROW_TILE
    row = lambda width: pl.BlockSpec((tm, width), lambda i: (i, 0))
    full = lambda a: pl.BlockSpec(a.shape, lambda i: (0,) * a.ndim)
    return pl.pallas_call(
        functools.partial(_outproj_kernel, len(acts)),
        grid=(rows // tm,),
        in_specs=[row(D_MODEL)] + [row(a.shape[1]) for a in acts] + [full(w) for w in ws],
        out_specs=row(D_MODEL),
        out_shape=jax.ShapeDtypeStruct((rows, D_MODEL), F32),
        compiler_params=pltpu.CompilerParams(dimension_semantics=("arbitrary",), vmem_limit_bytes=VMEM_LIMIT),
        name="outproj",
    )(x, *acts, *ws)


def _ffn_kernel(x_ref, g_ref, wg_ref, wu_ref, wd_ref, o_ref, hn_ref, acc_ref):
    j = pl.program_id(1)

    @pl.when(j == 0)
    def _():
        x = x_ref[...]
        hn_ref[...] = _rms(x, g_ref[...]).astype(BF16)
        acc_ref[...] = x

    h = hn_ref[...]
    a = (_silu(_dot(h, wg_ref[...])) * _dot(h, wu_ref[...])).astype(BF16)
    acc_ref[...] += _dot(a, wd_ref[...])

    @pl.when(j == pl.num_programs(1) - 1)
    def _():
        o_ref[...] = acc_ref[...]


def _ffn(x, g, wg, wu, wd):
    rows = x.shape[0]
    d_ff = wg.shape[1]
    tm = FFN_ROW_TILE if rows % FFN_ROW_TILE == 0 else ROW_TILE
    tf = FFN_COL_TILE
    return pl.pallas_call(
        _ffn_kernel,
        grid=(rows // tm, d_ff // tf),
        in_specs=[
            pl.BlockSpec((tm, D_MODEL), lambda i, j: (i, 0)),
            pl.BlockSpec((1, D_MODEL), lambda i, j: (0, 0)),
            pl.BlockSpec((D_MODEL, tf), lambda i, j: (0, j)),
            pl.BlockSpec((D_MODEL, tf), lambda i, j: (0, j)),
            pl.BlockSpec((tf, D_MODEL), lambda i, j: (j, 0)),
        ],
        out_specs=pl.BlockSpec((tm, D_MODEL), lambda i, j: (i, 0)),
        out_shape=jax.ShapeDtypeStruct((rows, D_MODEL), F32),
        scratch_shapes=[pltpu.VMEM((tm, D_MODEL), BF16), pltpu.VMEM((tm, D_MODEL), F32)],
        compiler_params=pltpu.CompilerParams(dimension_semantics=("arbitrary", "arbitrary"), vmem_limit_bytes=VMEM_LIMIT),
        name="ffn",
    )(x, g, wg, wu, wd)


def _memkv_kernel(m_ref, wk_ref, wv_ref, bd_ref, kw_ref, k_ref, v_ref):
    mb = m_ref[...].astype(BF16)
    k_ref[...] = _headnorm_bd(_dot(mb, wk_ref[...]), bd_ref[...], kw_ref[...], MEM_HD)
    v_ref[...] = _dot(mb, wv_ref[...])


def _memkv(mem, wk, wv, kw):
    rows = mem.shape[0]
    bd = _block_diag_ones(MEM_W, MEM_HD)
    return pl.pallas_call(
        _memkv_kernel,
        out_shape=[jax.ShapeDtypeStruct((rows, MEM_W), F32)] * 2,
        compiler_params=pltpu.CompilerParams(vmem_limit_bytes=VMEM_LIMIT),
        name="memkv",
    )(mem, wk, wv, bd, kw)


def _memattn_kernel(n_seq, n, x_ref, g_ref, wq_ref, wo_ref, bd_ref, qw_ref, mk_ref, mv_ref, o_ref, q_sc, a_sc):
    x = x_ref[...]
    h = _rms(x, g_ref[...]).astype(BF16)
    q = _headnorm_bd(_dot(h, wq_ref[...]), bd_ref[...], qw_ref[...], MEM_HD)
    q_sc[...] = q * MEM_HD ** -0.5

    def softmax(sc):
        p = jnp.exp(sc - jnp.max(sc, axis=-1, keepdims=True))
        return (p / jnp.sum(p, axis=-1, keepdims=True)).astype(BF16)

    def one_seq(s, carry):
        r0 = pl.multiple_of(s * n, n)
        qs = q_sc[pl.ds(r0, n), :].astype(BF16)
        mk = mk_ref[s].astype(BF16)
        mv = mv_ref[s].astype(BF16)
        outs = []
        for hh in range(MEM_HEADS):
            sl = slice(hh * MEM_HD, (hh + 1) * MEM_HD)
            outs.append(_dot(softmax(_dot_nt(qs[:, sl], mk[:, sl])), mv[:, sl]))
        a_sc[pl.ds(r0, n), :] = jnp.concatenate(outs, axis=1)
        return carry

    def one_short_seq(own, s, carry):
        r0 = pl.multiple_of(s * n, n)
        qs = jnp.where(own, jnp.concatenate([q_sc[pl.ds(r0, n), :]] * MEM_HEADS, axis=0), 0.0).astype(BF16)
        pv = _dot(softmax(_dot_nt(qs, mk_ref[s].astype(BF16))), mv_ref[s].astype(BF16))
        pv = jnp.where(own, pv, 0.0)
        a_sc[pl.ds(r0, n), :] = sum(pv[hh * n:(hh + 1) * n] for hh in range(MEM_HEADS))
        return carry

    if n_seq > 1:
        own = (lax.broadcasted_iota(jnp.int32, (MEM_HEADS * n, MEM_W), 0) // n
               == lax.broadcasted_iota(jnp.int32, (MEM_HEADS * n, MEM_W), 1) // MEM_HD)
        lax.fori_loop(0, n_seq, functools.partial(one_short_seq, own), 0, unroll=SEQ_UNROLL)
    else:
        one_seq(0, 0)
    o_ref[...] = x + _dot(a_sc[...].astype(BF16), wo_ref[...])


def _memattn(x, g, wq, wo, qw, mk, mv, row0, n_mem, mem0, n_seq, n, steps_per_mem):
    tm = n_seq * n
    steps = n_mem // n_seq * steps_per_mem
    blk0 = row0 // tm
    memblk0 = mem0 // n_seq
    bd = _block_diag_ones(MEM_W, MEM_HD)
    full = lambda a: pl.BlockSpec(a.shape, lambda i: (0,) * a.ndim)
    row_spec = pl.BlockSpec((tm, D_MODEL), lambda i: (blk0 + i, 0))
    mem_spec = pl.BlockSpec((n_seq, MEM_LEN, MEM_W), lambda i: (memblk0 + i // steps_per_mem, 0, 0))
    return pl.pallas_call(
        functools.partial(_memattn_kernel, n_seq, n),
        grid=(steps,),
        in_specs=[row_spec, full(g), full(wq), full(wo), full(bd), full(qw), mem_spec, mem_spec],
        out_specs=row_spec,
        out_shape=jax.ShapeDtypeStruct(x.shape, F32),
        input_output_aliases={0: 0},
        scratch_shapes=[pltpu.VMEM((tm, MEM_W), F32), pltpu.VMEM((tm, MEM_W), F32)],
        compiler_params=pltpu.CompilerParams(dimension_semantics=("arbitrary",), vmem_limit_bytes=VMEM_LIMIT),
        name="memattn",
    )(x, g, wq, wo, bd, qw, mk, mv)


def _sink_softmax(s, sink):
    m = jnp.maximum(jnp.max(s, axis=-1, keepdims=True), sink)
    p = jnp.exp(s - m)
    return p / (jnp.sum(p, axis=-1, keepdims=True) + jnp.exp(sink - m))


def _swa_prompt_kernel(sink_ref, q_ref, kc_ref, kp_ref, vc_ref, vp_ref, bias_ref, o_ref):
    i = pl.program_id(1)
    tq = q_ref.shape[0]
    col = lax.broadcasted_iota(jnp.int32, (tq, 2 * tq), 1)
    first_valid = jnp.where(i > 0, 0, tq)
    colmask = col >= first_valid
    q = q_ref[...]
    outs = []
    for kv in range(ATT_KV):
        sl = slice(kv * HD_A, (kv + 1) * HD_A)
        kb = jnp.concatenate([kp_ref[:, sl], kc_ref[:, sl]], axis=0).astype(BF16)
        vb = jnp.concatenate([vp_ref[:, sl], vc_ref[:, sl]], axis=0).astype(BF16)
        for gg in range(ATT_GROUP):
            hh = kv * ATT_GROUP + gg
            s = _dot_nt(q[:, hh * HD_A:(hh + 1) * HD_A], kb) + bias_ref[hh]
            s = jnp.where(colmask, s, NEG)
            outs.append(_dot(_sink_softmax(s, sink_ref[hh]).astype(BF16), vb))
    o_ref[...] = jnp.concatenate(outs, axis=1).astype(BF16)


def _swa_prompt(sinks, q, k, v, bias, bsz, n):
    tq = WINDOW
    nq = n // tq
    cur = lambda width: pl.BlockSpec((tq, width), lambda b, i: (b * nq + i, 0))
    prev = lambda width: pl.BlockSpec((tq, width), lambda b, i: (b * nq + jnp.maximum(i - 1, 0), 0))
    return pl.pallas_call(
        _swa_prompt_kernel,
        grid=(bsz, nq),
        in_specs=[pl.BlockSpec(memory_space=pltpu.SMEM), cur(E_Q), cur(E_KV), prev(E_KV), cur(E_KV), prev(E_KV),
                  pl.BlockSpec(bias.shape, lambda b, i: (0, 0, 0))],
        out_specs=cur(E_Q),
        out_shape=jax.ShapeDtypeStruct((bsz * n, E_Q), BF16),
        compiler_params=pltpu.CompilerParams(dimension_semantics=("arbitrary", "arbitrary"), vmem_limit_bytes=VMEM_LIMIT),
        name="swa_prompt",
    )(sinks, q, k, k, v, v, bias)


def _swa_sample_kernel(n_seq, q_ref, wk_ref, wv_ref, kn_ref, vn_ref, bias_ref, sink_ref, o_ref):
    rows = q_ref.shape[1]
    pad = jnp.zeros((WINDOW - kn_ref.shape[1], E_KV), F32)
    first = lax.broadcasted_iota(jnp.int32, (rows, HD_A), 0) < rows // ATT_KV
    bias = bias_ref[...]
    sink = sink_ref[...]

    def one_seq(s, carry):
        kall = jnp.concatenate([wk_ref[s], kn_ref[s], pad], axis=0).astype(BF16)
        vall = jnp.concatenate([wv_ref[s], vn_ref[s], pad], axis=0).astype(BF16)
        w = _sink_softmax(_dot_nt(q_ref[s], kall) + bias, sink).astype(BF16)
        pv = _dot(w, vall)
        o_ref[s] = jnp.where(first, pv[:, :HD_A], pv[:, HD_A:])
        return carry

    lax.fori_loop(0, n_seq, one_seq, 0, unroll=SEQ_UNROLL)


def _swa_sample(q, wk, wv, kn, vn, bias, sink_col, n_seq):
    sb, rows, _ = q.shape
    lead = lambda a: pl.BlockSpec((n_seq,) + a.shape[1:], lambda i: (i, 0, 0))
    full = lambda a: pl.BlockSpec(a.shape, lambda i: (0,) * a.ndim)
    return pl.pallas_call(
        functools.partial(_swa_sample_kernel, n_seq),
        grid=(sb // n_seq,),
        in_specs=[lead(q), lead(wk), lead(wv), lead(kn), lead(vn), full(bias), full(sink_col)],
        out_specs=pl.BlockSpec((n_seq, rows, HD_A), lambda i: (i, 0, 0)),
        out_shape=jax.ShapeDtypeStruct((sb, rows, HD_A), F32),
        compiler_params=pltpu.CompilerParams(dimension_semantics=("arbitrary",), vmem_limit_bytes=VMEM_LIMIT),
        name="swa_sample",
    )(q, wk, wv, kn, vn, bias, sink_col)


def _ssd_chunk(xbc, prev, z, dtr, dt_t, hst, cw, cb, dtb, dtb_t, alog, alog_t, dsk, gnw):
    c = xbc.shape[0]
    row = lax.broadcasted_iota(jnp.int32, (c, CONV_DIM), 0)
    conv = cb + xbc * cw[CONV_W - 1:CONV_W]
    for s in range(1, CONV_W):
        shifted = jnp.where(row >= s, pltpu.roll(xbc, s, 0), pltpu.roll(prev, s, 0))
        conv = conv + shifted * cw[CONV_W - 1 - s:CONV_W - s]
    xc = _silu(conv)
    gw = SSM_GROUPS * SSM_N
    xs, bm, cm = xc[:, :SSM_INNER], xc[:, SSM_INNER:SSM_INNER + gw], xc[:, SSM_INNER + gw:]

    lane = lax.broadcasted_iota(jnp.int32, (1, LANES), 1)
    a_row = jnp.where(lane < SSM_HEADS, -jnp.exp(alog), 0.0)
    dt = _softplus(dtr + dtb)
    dta = dt * a_row
    dta_t = _softplus(dt_t + dtb_t) * (-jnp.exp(alog_t))
    ti = lax.broadcasted_iota(jnp.int32, (c, c), 0)
    si = lax.broadcasted_iota(jnp.int32, (c, c), 1)
    tri = ti >= si
    lower = jnp.where(tri, 1.0, 0.0).astype(BF16)
    upper = jnp.where(ti <= si, 1.0, 0.0).astype(BF16)
    cs_col = sum(_dot(lower, part) for part in _split3(dta))
    cs_row = sum(_dot(part, upper) for part in _split3(dta_t))
    ecs = jnp.exp(cs_col)
    cs_last = cs_col[c - 1:c, :]
    tail = jnp.exp(cs_last - cs_col)
    elast = jnp.exp(cs_last)

    ys, hs = [], []
    hpg = SSM_HEADS // SSM_GROUPS
    for g in range(SSM_GROUPS):
        bg = bm[:, g * SSM_N:(g + 1) * SSM_N]
        cgb = cm[:, g * SSM_N:(g + 1) * SSM_N].astype(BF16)
        gmat = _dot_nt(cgb, bg.astype(BF16))
        for hh in range(g * hpg, (g + 1) * hpg):
            seg = cs_col[:, hh:hh + 1] - cs_row[hh:hh + 1, :]
            decay = jnp.exp(jnp.where(tri, seg, NEG))
            xdt = (xs[:, hh * SSM_P:(hh + 1) * SSM_P] * dt[:, hh:hh + 1]).astype(BF16)
            h_h = hst[hh * SSM_P:(hh + 1) * SSM_P, :]
            ys.append(_dot((gmat * decay).astype(BF16), xdt) + _dot_nt(cgb, h_h.astype(BF16)) * ecs[:, hh:hh + 1])
            hs.append(h_h * elast[:, hh:hh + 1] + _dot_tn(xdt, (bg * tail[:, hh:hh + 1]).astype(BF16)))
    y = jnp.concatenate(ys, axis=1) + xs * dsk
    y = y * _silu(z)
    gn = SSM_INNER // SSM_GROUPS
    y = jnp.concatenate([_rms(y[:, g * gn:(g + 1) * gn], gnw[:, g * gn:(g + 1) * gn]) for g in range(SSM_GROUPS)], axis=1)
    return y, jnp.concatenate(hs, axis=0)


def _ssd_prompt_kernel(xbc_ref, prev_ref, z_ref, dt_ref, dtt_ref, cw_ref, cb_ref, dtb_ref, dtbt_ref, alog_ref, alogt_ref,
                       dsk_ref, gnw_ref, y_ref, hl_ref, hs_ref):
    c = pl.program_id(1)

    @pl.when(c == 0)
    def _():
        hs_ref[...] = jnp.zeros(hs_ref.shape, F32)

    prev = prev_ref[...] * (c > 0).astype(F32)
    y, hn = _ssd_chunk(xbc_ref[...], prev, z_ref[...], dt_ref[...], dtt_ref[0], hs_ref[...], cw_ref[...], cb_ref[...],
                       dtb_ref[...], dtbt_ref[...], alog_ref[...], alogt_ref[...], dsk_ref[...], gnw_ref[...])
    y_ref[...] = y.astype(BF16)
    hs_ref[...] = hn
    hl_ref[0] = hn


def _ssd_prompt(xbc, z, dt, dt_t, params, bsz, n):
    c = SSD_CHUNK
    nc = n // c
    cur = lambda width: pl.BlockSpec((c, width), lambda b, i: (b * nc + i, 0))
    full = lambda a: pl.BlockSpec(a.shape, lambda b, i: (0,) * a.ndim)
    hd = SSM_HEADS * SSM_P
    return pl.pallas_call(
        _ssd_prompt_kernel,
        grid=(bsz, nc),
        in_specs=[cur(CONV_DIM), pl.BlockSpec((c, CONV_DIM), lambda b, i: (b * nc + jnp.maximum(i - 1, 0), 0)),
                  cur(SSM_INNER), cur(LANES), pl.BlockSpec((1, SSM_HEADS, c), lambda b, i: (b, 0, i))]
        + [full(p) for p in params],
        out_specs=[cur(SSM_INNER), pl.BlockSpec((1, hd, SSM_N), lambda b, i: (b, 0, 0))],
        out_shape=[jax.ShapeDtypeStruct((bsz * n, SSM_INNER), BF16), jax.ShapeDtypeStruct((bsz, hd, SSM_N), F32)],
        scratch_shapes=[pltpu.VMEM((hd, SSM_N), F32)],
        compiler_params=pltpu.CompilerParams(dimension_semantics=("arbitrary", "arbitrary"), vmem_limit_bytes=VMEM_LIMIT),
        name="ssd_prompt",
    )(xbc, xbc, z, dt, dt_t, *params)


def _ssd_sample_kernel(n_seq, xbc_ref, st_ref, z_ref, dt_ref, dtt_ref, h0_ref, cw_ref, cb_ref, dtb_ref, dtbt_ref,
                       alog_ref, alogt_ref, dsk_ref, gnw_ref, y_ref, hn_ref):
    def one_seq(s, carry):
        y, hn = _ssd_chunk(xbc_ref[s], st_ref[s], z_ref[s], dt_ref[s], dtt_ref[s], h0_ref[s], cw_ref[...], cb_ref[...],
                           dtb_ref[...], dtbt_ref[...], alog_ref[...], alogt_ref[...], dsk_ref[...], gnw_ref[...])
        y_ref[s] = y
        hn_ref[s] = hn
        return carry

    lax.fori_loop(0, n_seq, one_seq, 0)


def _ssd_sample(xbc, st, z, dt, dt_t, h0, params, n_seq):
    sb = xbc.shape[0]
    lead = lambda a: pl.BlockSpec((n_seq,) + a.shape[1:], lambda i: (i, 0, 0))
    full = lambda a: pl.BlockSpec(a.shape, lambda i: (0,) * a.ndim)
    seq = (xbc, st, z, dt, dt_t, h0)
    return pl.pallas_call(
        functools.partial(_ssd_sample_kernel, n_seq),
        grid=(sb // n_seq,),
        in_specs=[lead(a) for a in seq] + [full(p) for p in params],
        out_specs=[lead(z), lead(h0)],
        out_shape=[jax.ShapeDtypeStruct(z.shape, F32), jax.ShapeDtypeStruct(h0.shape, F32)],
        compiler_params=pltpu.CompilerParams(dimension_semantics=("arbitrary",), vmem_limit_bytes=VMEM_LIMIT),
        name="ssd_sample",
    )(*seq, *params)


def _moba_prompt_kernel(q_ref, k_ref, v_ref, km_ref, down_ref, dprev_ref, cfar_ref, o_ref,
                        qa_ref, s_ref, p_ref, m_ref, al_ref, acc_ref):
    cur = pl.program_id(2)
    tq = q_ref.shape[0]
    rows = ATT_GROUP * tq
    qs = jnp.concatenate([q_ref[:, g * HD_C:(g + 1) * HD_C] for g in range(ATT_GROUP)], axis=0)
    lane = lax.broadcasted_iota(jnp.int32, (rows, LANES), 1)
    gate = jnp.where(lane < cur, _dot_nt(qs, km_ref[0, 0]), NEG)
    sel = _top3_mask(gate, lane.astype(F32))
    allowed = jnp.where(lane < cur, sel, jnp.where(lane == cur, 1.0, 0.0))
    c = cfar_ref[0]
    c_hi = c.astype(BF16)
    c_lo = (c - c_hi.astype(F32)).astype(BF16)
    upper = jnp.where(lane == CONST_LANE, c_hi, jnp.where(lane == CONST_LANE + 1, c_lo,
                                                         jnp.where(allowed > 0.5, 0.0, NEG).astype(BF16)))
    qa_ref[:, :HD_C] = qs
    qa_ref[:, HD_C:] = upper
    m_ref[...] = jnp.full(m_ref.shape, LOWEST, F32)
    acc_ref[...] = jnp.zeros(acc_ref.shape, F32)

    def attend(j, delta_ref):
        r0 = pl.multiple_of(j * MOBA_BLOCK, MOBA_BLOCK)
        s_ref[...] = _dot_nt(qa_ref[...], k_ref[pl.ds(r0, MOBA_BLOCK), :])
        for r in range(rows // SUB_ROWS):
            sl = slice(r * SUB_ROWS, (r + 1) * SUB_ROWS)
            halves = [s_ref[sl, hf * LANES:(hf + 1) * LANES] for hf in range(2)]
            if delta_ref is not None:
                halves = [sh + delta_ref[0, sl, hf * LANES:(hf + 1) * LANES] for hf, sh in enumerate(halves)]
            m_old = m_ref[sl, :]
            m_new = jnp.maximum(m_old, jnp.max(jnp.maximum(halves[0], halves[1]), axis=-1, keepdims=True))
            for hf, sh in enumerate(halves):
                p_ref[sl, hf * LANES:(hf + 1) * LANES] = jnp.exp2(sh - m_new).astype(BF16)
            al_ref[sl, :] = jnp.exp2(m_old - m_new)
            m_ref[sl, :] = m_new
        pv = _dot(p_ref[...], v_ref[pl.ds(r0, MOBA_BLOCK), :])
        al = al_ref[...]
        for hf in range(2):
            hs = slice(hf * LANES, (hf + 1) * LANES)
            acc_ref[:, hs] = al * acc_ref[:, hs] + pv[:, hs]

    attend(cur, down_ref)

    @pl.when(cur >= 1)
    def _():
        attend(cur - 1, dprev_ref)

    def far(j, carry):
        attend(j, None)
        return carry

    lax.fori_loop(0, jnp.maximum(cur - 1, 0), far, 0)
    o = acc_ref[:, :HD_C] / acc_ref[:, HD_C:]
    for g in range(ATT_GROUP):
        o_ref[:, g * HD_C:(g + 1) * HD_C] = o[g * tq:(g + 1) * tq].astype(BF16)


def _moba_prompt(q, kaug, vaug, kmean, down, dprev, cfar, bsz, n):
    tq = MOBA_BLOCK
    nq = n // tq
    assert nq <= CONST_LANE
    rows = ATT_GROUP * tq
    gw = ATT_GROUP * HD_C
    qspec = pl.BlockSpec((tq, gw), lambda b, kv, i: (b * nq + i, kv))
    kvspec = pl.BlockSpec((n, 2 * HD_C), lambda b, kv, i: (b, kv))
    tab = lambda a: pl.BlockSpec((1,) + a.shape[1:], lambda b, kv, i: (kv, 0, 0))
    return pl.pallas_call(
        _moba_prompt_kernel,
        grid=(bsz, ATT_KV, nq),
        in_specs=[qspec, kvspec, kvspec, pl.BlockSpec((1, 1, LANES, HD_C), lambda b, kv, i: (b, kv, 0, 0)),
                  tab(down), tab(dprev), tab(cfar)],
        out_specs=qspec,
        out_shape=jax.ShapeDtypeStruct((bsz * n, C_Q), BF16),
        scratch_shapes=[pltpu.VMEM((rows, 2 * HD_C), BF16), pltpu.VMEM((rows, 2 * HD_C), F32),
                        pltpu.VMEM((rows, 2 * HD_C), BF16), pltpu.VMEM((rows, LANES), F32),
                        pltpu.VMEM((rows, LANES), F32), pltpu.VMEM((rows, 2 * HD_C), F32)],
        compiler_params=pltpu.CompilerParams(dimension_semantics=("arbitrary",) * 3, vmem_limit_bytes=VMEM_LIMIT),
        name="moba_prompt",
    )(q, kaug, vaug, kmean, down, dprev, cfar)


def _moba_sample_kernel(nbs, ppb, pt_ref, q_ref, *refs):
    npg = nbs * ppb
    k_pages, v_pages = refs[:npg], refs[npg:2 * npg]
    kn_ref, vn_ref, tbl_ref, bown_ref, o_ref, gate_sc, m_sc, l_sc, o_sc = refs[2 * npg:]
    step = pl.program_id(1)
    nblk = pl.num_programs(1) * nbs
    rows = q_ref.shape[1]
    lane = lax.broadcasted_iota(jnp.int32, (rows, LANES), 1)

    @pl.when(step == 0)
    def _():
        gate_sc[...] = jnp.zeros(gate_sc.shape, F32)
        m_sc[...] = jnp.zeros(m_sc.shape, F32)
        l_sc[...] = jnp.zeros(l_sc.shape, F32)

    q = q_ref[0]
    half = rows // ATT_KV
    lane_h = lax.broadcasted_iota(jnp.int32, (half, LANES), 1)
    for kv in range(ATT_KV):
        rs = slice(kv * half, (kv + 1) * half)
        own = lambda pages: jnp.concatenate([r[pl.ds(kv, PAGE_SIZE, stride=ATT_KV), :] for r in pages], axis=0)
        kk, vv = own(k_pages), own(v_pages)
        qk = q[rs]
        qf = qk.astype(F32)
        s_all = _dot_nt(qk, kk.astype(BF16))
        gate_n, m_n, l_n = gate_sc[rs, :], m_sc[rs, :], l_sc[rs, :]
        for i in range(nbs):
            j = step * nbs + i
            cs = slice(i * MOBA_BLOCK, (i + 1) * MOBA_BLOCK)
            kmean = (jnp.sum(kk[cs], axis=0, keepdims=True) * (1.0 / MOBA_BLOCK)).astype(BF16).astype(F32)
            gate = jnp.sum(qf * kmean, axis=-1, keepdims=True)
            s = s_all[:, cs] + tbl_ref[jnp.where(j == nblk - 1, 1, 0), rs, :]
            m = jnp.max(s, axis=-1, keepdims=True)
            p = jnp.exp2(s - m)
            hit = lane_h == j
            gate_n = jnp.where(hit, gate, gate_n)
            m_n = jnp.where(hit, m, m_n)
            l_n = jnp.where(hit, jnp.sum(p, axis=-1, keepdims=True), l_n)
            o_sc[j, rs, :] = _dot(p.astype(BF16), vv[cs].astype(BF16))
        gate_sc[rs, :], m_sc[rs, :], l_sc[rs, :] = gate_n, m_n, l_n

    @pl.when(step == pl.num_programs(1) - 1)
    def _():
        pad = jnp.zeros((LANES - kn_ref.shape[1], HD_C), F32)
        sel = _top3_mask(jnp.where(lane < nblk, gate_sc[...], LOWEST), lane.astype(F32)) > 0.5
        m_blk = jnp.where(sel, m_sc[...], LOWEST)
        kn = jnp.concatenate([kn_ref[0], pad], axis=0).astype(BF16)
        vn = jnp.concatenate([vn_ref[0], pad], axis=0).astype(BF16)
        s_own = _dot_nt(q, kn) + bown_ref[...]
        m_tot = jnp.maximum(jnp.max(m_blk, axis=-1, keepdims=True), jnp.max(s_own, axis=-1, keepdims=True))
        w = jnp.exp2(m_blk - m_tot)
        p_own = jnp.exp2(s_own - m_tot)
        den = jnp.sum(w * l_sc[...], axis=-1, keepdims=True) + jnp.sum(p_own, axis=-1, keepdims=True)
        o = _dot(p_own.astype(BF16), vn)
        for b in range(o_sc.shape[0]):
            o = o + w[:, b:b + 1] * o_sc[b]
        o_ref[0] = o / den


def _moba_sample(pt, q, pool_k, pool_v, kn, vn, tbl, bown, n_pages):
    sb, rows, _ = q.shape
    ppb = MOBA_BLOCK // PAGE_SIZE
    nblk = n_pages // ppb
    nbs = SAMPLE_BLOCKS_PER_STEP
    assert nblk % nbs == 0 and nblk <= LANES

    def page(p):
        return pl.BlockSpec((PAGE_SIZE * ATT_KV, HD_C), lambda b, s, pt_ref: (pt_ref[b * n_pages + s * nbs * ppb + p], 0))

    seq = lambda a: pl.BlockSpec((1,) + a.shape[1:], lambda b, s, pt_ref: (b,) + (0,) * (a.ndim - 1))
    full = lambda a: pl.BlockSpec(a.shape, lambda b, s, pt_ref: (0,) * a.ndim)
    pages = [page(p) for p in range(nbs * ppb)]
    grid_spec = pltpu.PrefetchScalarGridSpec(
        num_scalar_prefetch=1,
        grid=(sb, nblk // nbs),
        in_specs=[seq(q)] + pages * 2 + [seq(kn), seq(vn), full(tbl), full(bown)],
        out_specs=seq(q),
        scratch_shapes=[pltpu.VMEM((rows, LANES), F32)] * 3 + [pltpu.VMEM((nblk, rows, HD_C), F32)],
    )
    return pl.pallas_call(
        functools.partial(_moba_sample_kernel, nbs, ppb),
        grid_spec=grid_spec,
        out_shape=jax.ShapeDtypeStruct(q.shape, F32),
        compiler_params=pltpu.CompilerParams(dimension_semantics=("arbitrary", "arbitrary"), vmem_limit_bytes=VMEM_LIMIT),
        name="moba_sample",
    )(pt, q, *([pool_k] * len(pages)), *([pool_v] * len(pages)), kn, vn, tbl, bown)


def _bias_table(tab_t, head, dist, valid):
    rows_tab = jnp.take(tab_t, jnp.asarray(np.asarray(head, np.int32)), axis=0)
    bucket = jnp.asarray(_bucket_np(np.broadcast_to(dist, np.broadcast_shapes(np.shape(dist), np.shape(valid)))))
    out = jnp.zeros(bucket.shape, F32)
    for b in range(N_BUCKETS):
        out = jnp.where(bucket == b, rows_tab[:, b:b + 1], out)
    return jnp.where(jnp.asarray(np.broadcast_to(valid, bucket.shape)), out, NEG)


def _head_rows(per_head):
    return np.repeat(np.arange(ATT_HEADS), per_head)


def _to_group_rows(a, sb, sn, hd):
    return a.reshape(sb, sn, ATT_KV, ATT_GROUP, hd).transpose(0, 2, 3, 1, 4).reshape(sb, ATT_KV, ATT_GROUP * sn, hd)


def _from_group_rows(a, sb, sn, hd):
    return a.reshape(sb, ATT_KV, ATT_GROUP, sn, hd).transpose(0, 3, 1, 2, 4).reshape(sb * sn, ATT_HEADS * hd)


def _pad_lanes(v):
    return jnp.pad(v.astype(F32), (0, LANES - v.shape[0]))[None]


def kernel(x_prompt, x_sample, cache_win_k, cache_win_v, state_conv, state_ssm, cache_moba_k, cache_moba_v, cache_mem_k, cache_mem_v, page_table, mem_prompt, rel_bias, norm_mix, norm_mem, norm_ffn, even_w_in, even_w_out, even_q_norm, even_k_norm, even_sinks, even_conv_w, even_conv_b, even_dt_bias, even_a_log, even_d_skip, even_gate_norm, odd_w_in, odd_w_out, odd_q_norm, odd_k_norm, mem_wq, mem_wk, mem_wv, mem_wo, mem_q_norm, mem_k_norm, ffn_w_gate, ffn_w_up, ffn_w_down):
    bsz, n = x_prompt.shape[:2]
    sb, sn = x_sample.shape[:2]
    depth = norm_mix.shape[0]
    rp, rs = bsz * n, sb * sn
    n_pages = page_table.shape[1]
    n_pool = cache_moba_k.shape[1]
    past_len = n_pages * PAGE_SIZE
    assert past_len % MOBA_BLOCK == 0 and cache_win_k.shape[2] == WINDOW and sn >= CONV_W - 1
    seq_group = 16

    x = jnp.concatenate([x_prompt.reshape(rp, D_MODEL), x_sample.reshape(rs, D_MODEL)], axis=0)
    tab_t = rel_bias.T.astype(F32)

    qi = np.tile(np.arange(WINDOW), ATT_HEADS)[:, None]
    kj = np.arange(2 * WINDOW)[None, :]
    d = WINDOW + qi - kj
    swa_bias_p = _bias_table(tab_t, _head_rows(WINDOW), d, (d >= 0) & (d < WINDOW)).reshape(ATT_HEADS, WINDOW, 2 * WINDOW)
    ti = np.tile(np.arange(sn), ATT_HEADS)[:, None]
    d = np.where(kj < WINDOW, WINDOW + ti - kj, ti - (kj - WINDOW))
    swa_bias_s = _bias_table(tab_t, _head_rows(sn), d, (d >= 0) & (d < WINDOW) & (kj < WINDOW + sn))
    assert _bucket_np(MOBA_BLOCK + 1) == N_BUCKETS - 1
    far_bias = tab_t[:, N_BUCKETS - 1] * LOG2E
    qi = np.tile(np.arange(MOBA_BLOCK), ATT_HEADS)[:, None]
    kj = np.arange(MOBA_BLOCK)[None, :]
    heads_p = _head_rows(MOBA_BLOCK)
    cfar_p = jnp.take(far_bias, jnp.asarray(heads_p))[:, None]
    grp = lambda t: t.reshape(ATT_KV, ATT_GROUP * MOBA_BLOCK, t.shape[-1])
    down_p = grp(jnp.where(jnp.asarray(qi >= kj), _bias_table(tab_t, heads_p, qi - kj, True) * LOG2E - cfar_p, NEG))
    dprev_p = grp(_bias_table(tab_t, heads_p, MOBA_BLOCK + qi - kj, True) * LOG2E - cfar_p)
    cfar_p = grp(jnp.broadcast_to(cfar_p, (cfar_p.shape[0], LANES)))
    heads_s = _head_rows(sn)
    row_kv = (heads_s // ATT_GROUP)[:, None]
    cfar_s = jnp.take(far_bias, jnp.asarray(heads_s))[:, None]
    tbl_s = jnp.stack([jnp.broadcast_to(cfar_s, (cfar_s.shape[0], MOBA_BLOCK)),
                       _bias_table(tab_t, heads_s, MOBA_BLOCK + ti - kj, True) * LOG2E])
    col = np.arange(LANES)[None, :]
    tok = col // ATT_KV
    bown_s = _bias_table(tab_t, heads_s, ti - tok, (row_kv == col % ATT_KV) & (ti >= tok) & (tok < sn)) * LOG2E

    pool_k = cache_moba_k.reshape(-1, HD_C)
    pool_v = cache_moba_v.reshape(-1, HD_C)
    pt_flat = page_table.reshape(-1)
    mem_k = cache_mem_k.reshape(depth * sb, MEM_LEN, MEM_W)
    mem_v = cache_mem_v.reshape(depth * sb, MEM_LEN, MEM_W)
    mem_rows = mem_prompt.reshape(bsz * MEM_LEN, D_MODEL)

    pwk, pwv, pcv, pss, pmk, pmv, pxk, pxv = [], [], [], [], [], [], [], []
    swk, swv, scv, sss, smk, smv = [], [], [], [], [], []
    for l in range(depth):
        g_mix = norm_mix[l][None]
        if l % 2 == 0:
            e = l // 2
            w_in = jnp.pad(even_w_in[e], ((0, 0), (0, EVEN_W - even_w_in.shape[2]))).astype(BF16)
            w_out = even_w_out[e].astype(BF16)
            qw = jnp.tile(even_q_norm[e], ATT_HEADS)[None]
            kw = jnp.tile(even_k_norm[e], ATT_KV)[None]
            q, k, v, z, xbc, dt = _proj_even(x, g_mix, w_in, qw, kw)
            params = (even_conv_w[e], even_conv_b[e][None], _pad_lanes(even_dt_bias[e]), even_dt_bias[e][:, None],
                      _pad_lanes(even_a_log[e]), even_a_log[e][:, None], jnp.repeat(even_d_skip[e], SSM_P)[None],
                      even_gate_norm[e][None])
            att_p = _swa_prompt(even_sinks[e], q, k, v, swa_bias_p, bsz, n)
            dt_t = dt[:rp, :SSM_HEADS].reshape(bsz, n, SSM_HEADS).transpose(0, 2, 1)
            y_p, h_p = _ssd_prompt(xbc, z, dt, dt_t, params, bsz, n)
            sink_col = jnp.repeat(even_sinks[e], sn)[:, None]
            k_s = k[rp:].reshape(sb, sn, E_KV)
            v_s = v[rp:].reshape(sb, sn, E_KV)
            qg = _to_group_rows(q[rp:], sb, sn, HD_A)
            zq = jnp.zeros_like(qg[:, 0])
            q_bd = jnp.concatenate([jnp.concatenate([qg[:, 0], zq], axis=-1),
                                    jnp.concatenate([zq, qg[:, 1]], axis=-1)], axis=1)
            att_s = _swa_sample(q_bd, cache_win_k[e].reshape(sb, WINDOW, E_KV), cache_win_v[e].reshape(sb, WINDOW, E_KV),
                                k_s, v_s, swa_bias_s, sink_col, seq_group)
            xbc_s = xbc[rp:].reshape(sb, sn, CONV_DIM)
            st = jnp.pad(state_conv[e], ((0, 0), (sn - (CONV_W - 1), 0), (0, 0)))
            dt_s = dt[rp:].reshape(sb, sn, LANES)
            y_s, h_s = _ssd_sample(xbc_s, st, z[rp:].reshape(sb, sn, SSM_INNER), dt_s,
                                   dt_s[:, :, :SSM_HEADS].transpose(0, 2, 1),
                                   state_ssm[e].reshape(sb, SSM_HEADS * SSM_P, SSM_N), params, seq_group)
            att = jnp.concatenate([att_p, _from_group_rows(att_s, sb, sn, HD_A).astype(BF16)], axis=0)
            yy = jnp.concatenate([y_p, y_s.reshape(rs, SSM_INNER).astype(BF16)], axis=0)
            x = _outproj(x, [att, yy], [w_out[:E_Q], w_out[E_Q:]])
            tail = lambda a, cnt: jnp.stack([a[(b + 1) * n - cnt:(b + 1) * n] for b in range(bsz)])
            pwk.append(tail(k, WINDOW).reshape(bsz, WINDOW, ATT_KV, HD_A))
            pwv.append(tail(v, WINDOW).reshape(bsz, WINDOW, ATT_KV, HD_A))
            pcv.append(tail(xbc, CONV_W - 1))
            pss.append(h_p.reshape(bsz, SSM_HEADS, SSM_P, SSM_N))
            swk.append(jnp.concatenate([cache_win_k[e], k_s.reshape(sb, sn, ATT_KV, HD_A)], axis=1)[:, -WINDOW:])
            swv.append(jnp.concatenate([cache_win_v[e], v_s.reshape(sb, sn, ATT_KV, HD_A)], axis=1)[:, -WINDOW:])
            scv.append(jnp.concatenate([state_conv[e], xbc_s], axis=1)[:, -(CONV_W - 1):])
            sss.append(h_s.reshape(sb, SSM_HEADS, SSM_P, SSM_N))
        else:
            o = l // 2
            w_in = odd_w_in[o].astype(BF16)
            w_out = odd_w_out[o].astype(BF16)
            nb = n // MOBA_BLOCK
            q, k, v, kaug, vaug, km = _proj_odd(x, g_mix, w_in, odd_q_norm[o][None], odd_k_norm[o][None], nb)
            kmean = km[:bsz * nb, 0].reshape(bsz, nb, ATT_KV, HD_C).transpose(0, 2, 1, 3)
            kmean = jnp.pad(kmean, ((0, 0), (0, 0), (0, LANES - nb), (0, 0))).astype(BF16)
            att_p = _moba_prompt(q, kaug, vaug, kmean, down_p, dprev_p, cfar_p, bsz, n)
            k_s = k[rp:].reshape(sb, sn, C_KV)
            v_s = v[rp:].reshape(sb, sn, C_KV)
            q_s = _to_group_rows(q[rp:], sb, sn, HD_C).reshape(sb, ATT_HEADS * sn, HD_C)
            att_s = _moba_sample(pt_flat + o * n_pool, q_s, pool_k, pool_v, k_s.reshape(sb, sn * ATT_KV, HD_C),
                                 v_s.reshape(sb, sn * ATT_KV, HD_C), tbl_s, bown_s, n_pages)
            att = jnp.concatenate([att_p, _from_group_rows(att_s, sb, sn, HD_C).astype(BF16)], axis=0)
            x = _outproj(x, [att], [w_out])
            pmk.append(k[:rp].reshape(bsz, n, ATT_KV, HD_C))
            pmv.append(v[:rp].reshape(bsz, n, ATT_KV, HD_C))
            smk.append(k_s.reshape(sb, sn, ATT_KV, HD_C))
            smv.append(v_s.reshape(sb, sn, ATT_KV, HD_C))
        mqw = jnp.tile(mem_q_norm[l], MEM_HEADS)[None]
        mkw = jnp.tile(mem_k_norm[l], MEM_HEADS)[None]
        wq = mem_wq[l].astype(BF16)
        wo = mem_wo[l].astype(BF16)
        xk, xv = _memkv(mem_rows, mem_wk[l].astype(BF16), mem_wv[l].astype(BF16), mkw)
        pxk.append(xk.reshape(bsz, MEM_LEN, MEM_HEADS, MEM_HD))
        pxv.append(xv.reshape(bsz, MEM_LEN, MEM_HEADS, MEM_HD))
        g_mem = norm_mem[l][None]
        x = _memattn(x, g_mem, wq, wo, mqw, xk.reshape(bsz, MEM_LEN, MEM_W), xv.reshape(bsz, MEM_LEN, MEM_W),
                     0, bsz, 0, 1, ROW_TILE, n // ROW_TILE)
        x = _memattn(x, g_mem, wq, wo, mqw, mem_k, mem_v, rp, sb, l * sb, seq_group, sn, 1)
        x = _ffn(x, norm_ffn[l][None], ffn_w_gate[l].astype(BF16), ffn_w_up[l].astype(BF16), ffn_w_down[l].astype(BF16))

    return (x[:rp].reshape(bsz, n, D_MODEL), x[rp:].reshape(sb, sn, D_MODEL),
            jnp.stack(pwk), jnp.stack(pwv), jnp.stack(pcv), jnp.stack(pss), jnp.stack(pmk), jnp.stack(pmv),
            jnp.stack(pxk), jnp.stack(pxv), jnp.stack(swk), jnp.stack(swv), jnp.stack(scv), jnp.stack(sss),
            jnp.stack(smk), jnp.stack(smv))
```

```python
import functools
import math

import jax
import jax.numpy as jnp
import numpy as np
from jax import lax
from jax.experimental import pallas as pl
from jax.experimental.pallas import tpu as pltpu

F32 = jnp.float32
BF16 = jnp.bfloat16

D_MODEL = 1024
ATT_HEADS = 8
ATT_KV = 2
ATT_GROUP = ATT_HEADS // ATT_KV
N_BUCKETS = 32
MAX_DIST = 128
WINDOW = 128
HD_A = 64
SSM_HEADS = 8
SSM_P = 64
SSM_INNER = SSM_HEADS * SSM_P
SSM_GROUPS = 2
SSM_N = 64
CONV_W = 4
CONV_DIM = SSM_INNER + 2 * SSM_GROUPS * SSM_N
SSD_CHUNK = 128
HD_C = 128
MOBA_BLOCK = 256
MOBA_TOPK = 3
PAGE_SIZE = 128
MEM_LEN = 256
MEM_HEADS = 4
MEM_HD = 64
MEM_W = MEM_HEADS * MEM_HD
E_Q = ATT_HEADS * HD_A
E_KV = ATT_KV * HD_A
C_Q = ATT_HEADS * HD_C
C_KV = ATT_KV * HD_C
EPS = 1e-6
NEG = -1e30
LOWEST = -3e38

LOG2E = 1.4426950408889634
LANES = 128
CONST_LANE = LANES - 2
SUB_ROWS = 128
MXU_ROW_GROUPS = 2
SAMPLE_BLOCKS_PER_STEP = 4
SEQ_UNROLL = 4
EVEN_W = E_Q + 2 * E_KV + SSM_INNER + CONV_DIM + LANES
ROW_TILE = 512
FFN_ROW_TILE = 1024
FFN_COL_TILE = 1408
VMEM_LIMIT = 56 * 1024 * 1024


def _dot(a, b):
    return jnp.dot(a, b, preferred_element_type=F32)


def _dot_nt(a, b):
    return lax.dot_general(a, b, (((1,), (1,)), ((), ())), preferred_element_type=F32)


def _dot_tn(a, b):
    return lax.dot_general(a, b, (((0,), (0,)), ((), ())), preferred_element_type=F32)


def _split3(x):
    hi = x.astype(BF16)
    r = x - hi.astype(F32)
    mid = r.astype(BF16)
    lo = (r - mid.astype(F32)).astype(BF16)
    return hi, mid, lo


def _rms(x, w):
    ms = jnp.mean(x * x, axis=-1, keepdims=True)
    return x * lax.rsqrt(ms + EPS) * w


def _headnorm_bd(y, ones_bd, w, hd):
    sq = y * y
    hi = sq.astype(BF16)
    lo = (sq - hi.astype(F32)).astype(BF16)
    ss = _dot(hi, ones_bd) + _dot(lo, ones_bd)
    return y * lax.rsqrt(ss * (1.0 / hd) + EPS) * w


def _silu(x):
    return x * jax.nn.sigmoid(x)


def _softplus(x):
    return jnp.maximum(x, 0.0) + jnp.log1p(jnp.exp(-jnp.abs(x)))


def _bucket_np(dist):
    n = np.maximum(np.asarray(dist, np.int64), 0)
    nf = np.maximum(n, 1).astype(np.float64)
    exact = N_BUCKETS // 2
    large = exact + np.floor(np.log(nf / exact) / math.log(MAX_DIST / exact) * (N_BUCKETS - exact) + 1e-6).astype(np.int64)
    return np.where(n < exact, n, np.minimum(large, N_BUCKETS - 1)).astype(np.int32)


def _block_diag_ones(width, hd):
    i = np.arange(width) // hd
    return jnp.asarray((i[:, None] == i[None, :]).astype(np.float32), BF16)


def _top3_mask(gate, lane_f):
    sel = jnp.zeros_like(gate)
    for _ in range(MOBA_TOPK):
        mx = jnp.max(gate, axis=-1, keepdims=True)
        idx = jnp.min(jnp.where(gate == mx, lane_f, 1e9), axis=-1, keepdims=True)
        hit = lane_f == idx
        sel = jnp.where(hit, 1.0, sel)
        gate = jnp.where(hit, LOWEST, gate)
    return sel


def _proj_even_kernel(x_ref, g_ref, w_ref, bdq_ref, bdk_ref, qw_ref, kw_ref,
                      q_ref, k_ref, v_ref, z_ref, xbc_ref, dt_ref):
    h = _rms(x_ref[...], g_ref[...]).astype(BF16)
    y = _dot(h, w_ref[...])
    q = _headnorm_bd(y[:, :E_Q], bdq_ref[...], qw_ref[...], HD_A)
    q_ref[...] = (q * HD_A ** -0.5).astype(BF16)
    o = E_Q
    k_ref[...] = _headnorm_bd(y[:, o:o + E_KV], bdk_ref[...], kw_ref[...], HD_A)
    o += E_KV
    v_ref[...] = y[:, o:o + E_KV]
    o += E_KV
    z_ref[...] = y[:, o:o + SSM_INNER]
    o += SSM_INNER
    xbc_ref[...] = y[:, o:o + CONV_DIM]
    o += CONV_DIM
    dt_ref[...] = y[:, o:o + LANES]


def _proj_even(x, g, w, qw, kw):
    rows = x.shape[0]
    tm = ROW_TILE
    row = lambda width: pl.BlockSpec((tm, width), lambda i: (i, 0))
    full = lambda a: pl.BlockSpec(a.shape, lambda i: (0,) * a.ndim)
    bdq = _block_diag_ones(E_Q, HD_A)
    bdk = _block_diag_ones(E_KV, HD_A)
    ins = (x, g, w, bdq, bdk, qw, kw)
    widths = (E_Q, E_KV, E_KV, SSM_INNER, CONV_DIM, LANES)
    dts = (BF16, F32, F32, F32, F32, F32)
    return pl.pallas_call(
        _proj_even_kernel,
        grid=(rows // tm,),
        in_specs=[row(D_MODEL)] + [full(a) for a in ins[1:]],
        out_specs=[row(wd) for wd in widths],
        out_shape=[jax.ShapeDtypeStruct((rows, wd), dt) for wd, dt in zip(widths, dts)],
        compiler_params=pltpu.CompilerParams(dimension_semantics=("arbitrary",), vmem_limit_bytes=VMEM_LIMIT),
        name="proj_even",
    )(*ins)


def _proj_odd_kernel(seq_blocks, x_ref, g_ref, w_ref, qw_ref, kw_ref,
                     q_ref, k_ref, v_ref, kaug_ref, vaug_ref, km_ref):
    i = pl.program_id(0)
    tm = x_ref.shape[0]
    h = _rms(x_ref[...], g_ref[...]).astype(BF16)
    y = _dot(h, w_ref[...])
    qw = qw_ref[...]
    for hh in range(ATT_HEADS):
        sl = slice(hh * HD_C, (hh + 1) * HD_C)
        q_ref[:, sl] = (_rms(y[:, sl], qw) * (HD_C ** -0.5 * LOG2E)).astype(BF16)
    nblk = tm // MOBA_BLOCK
    row = lax.broadcasted_iota(jnp.int32, (tm, LANES), 0)
    lane = lax.broadcasted_iota(jnp.int32, (tm, LANES), 1)
    blk = (i * nblk) % seq_blocks + sum(jnp.where(row >= b * MOBA_BLOCK, 1, 0) for b in range(1, nblk))
    onehot = jnp.where(jnp.logical_or(lane == blk, lane >= CONST_LANE), 1.0, 0.0).astype(BF16)
    ones = jnp.ones((tm, LANES), BF16)
    kw = kw_ref[...]
    for kv in range(ATT_KV):
        sl = slice(kv * HD_C, (kv + 1) * HD_C)
        kh = _rms(y[:, C_Q + kv * HD_C:C_Q + (kv + 1) * HD_C], kw)
        vh = y[:, C_Q + C_KV + kv * HD_C:C_Q + C_KV + (kv + 1) * HD_C]
        k_ref[:, sl] = kh
        v_ref[:, sl] = vh
        a = 2 * kv * HD_C
        kaug_ref[:, a:a + HD_C] = kh.astype(BF16)
        kaug_ref[:, a + HD_C:a + 2 * HD_C] = onehot
        vaug_ref[:, a:a + HD_C] = vh.astype(BF16)
        vaug_ref[:, a + HD_C:a + 2 * HD_C] = ones
        for b in range(nblk):
            km_ref[b, :, sl] = jnp.mean(kh[b * MOBA_BLOCK:(b + 1) * MOBA_BLOCK], axis=0, keepdims=True)


def _proj_odd(x, g, w, qw, kw, seq_blocks):
    rows = x.shape[0]
    tm = ROW_TILE
    nblk = tm // MOBA_BLOCK
    row = lambda width: pl.BlockSpec((tm, width), lambda i: (i, 0))
    full = lambda a: pl.BlockSpec(a.shape, lambda i: (0,) * a.ndim)
    ins = (x, g, w, qw, kw)
    widths = (C_Q, C_KV, C_KV, 2 * C_KV, 2 * C_KV)
    dts = (BF16, F32, F32, BF16, BF16)
    return pl.pallas_call(
        functools.partial(_proj_odd_kernel, seq_blocks),
        grid=(rows // tm,),
        in_specs=[row(D_MODEL)] + [full(a) for a in ins[1:]],
        out_specs=[row(wd) for wd in widths] + [pl.BlockSpec((nblk, 1, C_KV), lambda i: (i, 0, 0))],
        out_shape=[jax.ShapeDtypeStruct((rows, wd), dt) for wd, dt in zip(widths, dts)]
        + [jax.ShapeDtypeStruct((rows // MOBA_BLOCK, 1, C_KV), F32)],
        compiler_params=pltpu.CompilerParams(dimension_semantics=("arbitrary",), vmem_limit_bytes=VMEM_LIMIT),
        name="proj_odd",
    )(*ins)


def _outproj_kernel(n_in, x_ref, *refs):
    o_ref = refs[-1]
    acc = x_ref[...]
    for a_ref, w_ref in zip(refs[:n_in], refs[n_in:2 * n_in]):
        acc = acc + _dot(a_ref[...], w_ref[...])
    o_ref[...] = acc


def _outproj(x, acts, ws):
    rows = x.shape[0]
    tm = ROW_TILE
    row = lambda width: pl.BlockSpec((tm, width), lambda i: (i, 0))
    full = lambda a: pl.BlockSpec(a.shape, lambda i: (0,) * a.ndim)
    return pl.pallas_call(
        functools.partial(_outproj_kernel, len(acts)),
        grid=(rows // tm,),
        in_specs=[row(D_MODEL)] + [row(a.shape[1]) for a in acts] + [full(w) for w in ws],
        out_specs=row(D_MODEL),
        out_shape=jax.ShapeDtypeStruct((rows, D_MODEL), F32),
        compiler_params=pltpu.CompilerParams(dimension_semantics=("arbitrary",), vmem_limit_bytes=VMEM_LIMIT),
        name="outproj",
    )(x, *acts, *ws)


def _ffn_kernel(x_ref, g_ref, wg_ref, wu_ref, wd_ref, o_ref, hn_ref, acc_ref):
    j = pl.program_id(1)

    @pl.when(j == 0)
    def _():
        x = x_ref[...]
        hn_ref[...] = _rms(x, g_ref[...]).astype(BF16)
        acc_ref[...] = x

    h = hn_ref[...]
    a = (_silu(_dot(h, wg_ref[...])) * _dot(h, wu_ref[...])).astype(BF16)
    acc_ref[...] += _dot(a, wd_ref[...])

    @pl.when(j == pl.num_programs(1) - 1)
    def _():
        o_ref[...] = acc_ref[...]


def _ffn(x, g, wg, wu, wd):
    rows = x.shape[0]
    d_ff = wg.shape[1]
    tm = FFN_ROW_TILE if rows % FFN_ROW_TILE == 0 else ROW_TILE
    tf = FFN_COL_TILE
    return pl.pallas_call(
        _ffn_kernel,
        grid=(rows // tm, d_ff // tf),
        in_specs=[
            pl.BlockSpec((tm, D_MODEL), lambda i, j: (i, 0)),
            pl.BlockSpec((1, D_MODEL), lambda i, j: (0, 0)),
            pl.BlockSpec((D_MODEL, tf), lambda i, j: (0, j)),
            pl.BlockSpec((D_MODEL, tf), lambda i, j: (0, j)),
            pl.BlockSpec((tf, D_MODEL), lambda i, j: (j, 0)),
        ],
        out_specs=pl.BlockSpec((tm, D_MODEL), lambda i, j: (i, 0)),
        out_shape=jax.ShapeDtypeStruct((rows, D_MODEL), F32),
        scratch_shapes=[pltpu.VMEM((tm, D_MODEL), BF16), pltpu.VMEM((tm, D_MODEL), F32)],
        compiler_params=pltpu.CompilerParams(dimension_semantics=("arbitrary", "arbitrary"), vmem_limit_bytes=VMEM_LIMIT),
        name="ffn",
    )(x, g, wg, wu, wd)


def _memkv_kernel(m_ref, wk_ref, wv_ref, bd_ref, kw_ref, k_ref, v_ref):
    mb = m_ref[...].astype(BF16)
    k_ref[...] = _headnorm_bd(_dot(mb, wk_ref[...]), bd_ref[...], kw_ref[...], MEM_HD)
    v_ref[...] = _dot(mb, wv_ref[...])


def _memkv(mem, wk, wv, kw):
    rows = mem.shape[0]
    bd = _block_diag_ones(MEM_W, MEM_HD)
    return pl.pallas_call(
        _memkv_kernel,
        out_shape=[jax.ShapeDtypeStruct((rows, MEM_W), F32)] * 2,
        compiler_params=pltpu.CompilerParams(vmem_limit_bytes=VMEM_LIMIT),
        name="memkv",
    )(mem, wk, wv, bd, kw)


def _memattn_kernel(n_seq, n, x_ref, g_ref, wq_ref, wo_ref, bd_ref, qw_ref, mk_ref, mv_ref, o_ref, q_sc, a_sc):
    x = x_ref[...]
    h = _rms(x, g_ref[...]).astype(BF16)
    q = _headnorm_bd(_dot(h, wq_ref[...]), bd_ref[...], qw_ref[...], MEM_HD)
    q_sc[...] = q * MEM_HD ** -0.5

    def softmax(sc):
        p = jnp.exp(sc - jnp.max(sc, axis=-1, keepdims=True))
        return (p / jnp.sum(p, axis=-1, keepdims=True)).astype(BF16)

    def one_seq(s, carry):
        r0 = pl.multiple_of(s * n, n)
        qs = q_sc[pl.ds(r0, n), :].astype(BF16)
        mk = mk_ref[s].astype(BF16)
        mv = mv_ref[s].astype(BF16)
        outs = []
        for hh in range(MEM_HEADS):
            sl = slice(hh * MEM_HD, (hh + 1) * MEM_HD)
            outs.append(_dot(softmax(_dot_nt(qs[:, sl], mk[:, sl])), mv[:, sl]))
        a_sc[pl.ds(r0, n), :] = jnp.concatenate(outs, axis=1)
        return carry

    def one_short_seq(own, s, carry):
        r0 = pl.multiple_of(s * n, n)
        qs = jnp.where(own, jnp.concatenate([q_sc[pl.ds(r0, n), :]] * MEM_HEADS, axis=0), 0.0).astype(BF16)
        pv = _dot(softmax(_dot_nt(qs, mk_ref[s].astype(BF16))), mv_ref[s].astype(BF16))
        pv = jnp.where(own, pv, 0.0)
        a_sc[pl.ds(r0, n), :] = sum(pv[hh * n:(hh + 1) * n] for hh in range(MEM_HEADS))
        return carry

    if n_seq > 1:
        own = (lax.broadcasted_iota(jnp.int32, (MEM_HEADS * n, MEM_W), 0) // n
               == lax.broadcasted_iota(jnp.int32, (MEM_HEADS * n, MEM_W), 1) // MEM_HD)
        lax.fori_loop(0, n_seq, functools.partial(one_short_seq, own), 0, unroll=SEQ_UNROLL)
    else:
        one_seq(0, 0)
    o_ref[...] = x + _dot(a_sc[...].astype(BF16), wo_ref[...])


def _memattn(x, g, wq, wo, qw, mk, mv, row0, n_mem, mem0, n_seq, n, steps_per_mem):
    tm = n_seq * n
    steps = n_mem // n_seq * steps_per_mem
    blk0 = row0 // tm
    memblk0 = mem0 // n_seq
    bd = _block_diag_ones(MEM_W, MEM_HD)
    full = lambda a: pl.BlockSpec(a.shape, lambda i: (0,) * a.ndim)
    row_spec = pl.BlockSpec((tm, D_MODEL), lambda i: (blk0 + i, 0))
    mem_spec = pl.BlockSpec((n_seq, MEM_LEN, MEM_W), lambda i: (memblk0 + i // steps_per_mem, 0, 0))
    return pl.pallas_call(
        functools.partial(_memattn_kernel, n_seq, n),
        grid=(steps,),
        in_specs=[row_spec, full(g), full(wq), full(wo), full(bd), full(qw), mem_spec, mem_spec],
        out_specs=row_spec,
        out_shape=jax.ShapeDtypeStruct(x.shape, F32),
        input_output_aliases={0: 0},
        scratch_shapes=[pltpu.VMEM((tm, MEM_W), F32), pltpu.VMEM((tm, MEM_W), F32)],
        compiler_params=pltpu.CompilerParams(dimension_semantics=("arbitrary",), vmem_limit_bytes=VMEM_LIMIT),
        name="memattn",
    )(x, g, wq, wo, bd, qw, mk, mv)


def _sink_softmax(s, sink):
    m = jnp.maximum(jnp.max(s, axis=-1, keepdims=True), sink)
    p = jnp.exp(s - m)
    return p / (jnp.sum(p, axis=-1, keepdims=True) + jnp.exp(sink - m))


def _swa_prompt_kernel(sink_ref, q_ref, kc_ref, kp_ref, vc_ref, vp_ref, bias_ref, o_ref):
    i = pl.program_id(1)
    tq = q_ref.shape[0]
    col = lax.broadcasted_iota(jnp.int32, (tq, 2 * tq), 1)
    first_valid = jnp.where(i > 0, 0, tq)
    colmask = col >= first_valid
    q = q_ref[...]
    outs = []
    for kv in range(ATT_KV):
        sl = slice(kv * HD_A, (kv + 1) * HD_A)
        kb = jnp.concatenate([kp_ref[:, sl], kc_ref[:, sl]], axis=0).astype(BF16)
        vb = jnp.concatenate([vp_ref[:, sl], vc_ref[:, sl]], axis=0).astype(BF16)
        for gg in range(ATT_GROUP):
            hh = kv * ATT_GROUP + gg
            s = _dot_nt(q[:, hh * HD_A:(hh + 1) * HD_A], kb) + bias_ref[hh]
            s = jnp.where(colmask, s, NEG)
            outs.append(_dot(_sink_softmax(s, sink_ref[hh]).astype(BF16), vb))
    o_ref[...] = jnp.concatenate(outs, axis=1).astype(BF16)


def _swa_prompt(sinks, q, k, v, bias, bsz, n):
    tq = WINDOW
    nq = n // tq
    cur = lambda width: pl.BlockSpec((tq, width), lambda b, i: (b * nq + i, 0))
    prev = lambda width: pl.BlockSpec((tq, width), lambda b, i: (b * nq + jnp.maximum(i - 1, 0), 0))
    return pl.pallas_call(
        _swa_prompt_kernel,
        grid=(bsz, nq),
        in_specs=[pl.BlockSpec(memory_space=pltpu.SMEM), cur(E_Q), cur(E_KV), prev(E_KV), cur(E_KV), prev(E_KV),
                  pl.BlockSpec(bias.shape, lambda b, i: (0, 0, 0))],
        out_specs=cur(E_Q),
        out_shape=jax.ShapeDtypeStruct((bsz * n, E_Q), BF16),
        compiler_params=pltpu.CompilerParams(dimension_semantics=("arbitrary", "arbitrary"), vmem_limit_bytes=VMEM_LIMIT),
        name="swa_prompt",
    )(sinks, q, k, k, v, v, bias)


def _swa_sample_kernel(n_seq, q_ref, wk_ref, wv_ref, kn_ref, vn_ref, bias_ref, sink_ref, o_ref):
    rows = q_ref.shape[1]
    pad = jnp.zeros((WINDOW - kn_ref.shape[1], E_KV), F32)
    first = lax.broadcasted_iota(jnp.int32, (rows, HD_A), 0) < rows // ATT_KV
    bias = bias_ref[...]
    sink = sink_ref[...]

    def one_seq(s, carry):
        kall = jnp.concatenate([wk_ref[s], kn_ref[s], pad], axis=0).astype(BF16)
        vall = jnp.concatenate([wv_ref[s], vn_ref[s], pad], axis=0).astype(BF16)
        w = _sink_softmax(_dot_nt(q_ref[s], kall) + bias, sink).astype(BF16)
        pv = _dot(w, vall)
        o_ref[s] = jnp.where(first, pv[:, :HD_A], pv[:, HD_A:])
        return carry

    lax.fori_loop(0, n_seq, one_seq, 0, unroll=SEQ_UNROLL)


def _swa_sample(q, wk, wv, kn, vn, bias, sink_col, n_seq):
    sb, rows, _ = q.shape
    lead = lambda a: pl.BlockSpec((n_seq,) + a.shape[1:], lambda i: (i, 0, 0))
    full = lambda a: pl.BlockSpec(a.shape, lambda i: (0,) * a.ndim)
    return pl.pallas_call(
        functools.partial(_swa_sample_kernel, n_seq),
        grid=(sb // n_seq,),
        in_specs=[lead(q), lead(wk), lead(wv), lead(kn), lead(vn), full(bias), full(sink_col)],
        out_specs=pl.BlockSpec((n_seq, rows, HD_A), lambda i: (i, 0, 0)),
        out_shape=jax.ShapeDtypeStruct((sb, rows, HD_A), F32),
        compiler_params=pltpu.CompilerParams(dimension_semantics=("arbitrary",), vmem_limit_bytes=VMEM_LIMIT),
        name="swa_sample",
    )(q, wk, wv, kn, vn, bias, sink_col)


def _ssd_chunk(xbc, prev, z, dtr, dt_t, hst, cw, cb, dtb, dtb_t, alog, alog_t, dsk, gnw):
    c = xbc.shape[0]
    row = lax.broadcasted_iota(jnp.int32, (c, CONV_DIM), 0)
    conv = cb + xbc * cw[CONV_W - 1:CONV_W]
    for s in range(1, CONV_W):
        shifted = jnp.where(row >= s, pltpu.roll(xbc, s, 0), pltpu.roll(prev, s, 0))
        conv = conv + shifted * cw[CONV_W - 1 - s:CONV_W - s]
    xc = _silu(conv)
    gw = SSM_GROUPS * SSM_N
    xs, bm, cm = xc[:, :SSM_INNER], xc[:, SSM_INNER:SSM_INNER + gw], xc[:, SSM_INNER + gw:]

    lane = lax.broadcasted_iota(jnp.int32, (1, LANES), 1)
    a_row = jnp.where(lane < SSM_HEADS, -jnp.exp(alog), 0.0)
    dt = _softplus(dtr + dtb)
    dta = dt * a_row
    dta_t = _softplus(dt_t + dtb_t) * (-jnp.exp(alog_t))
    ti = lax.broadcasted_iota(jnp.int32, (c, c), 0)
    si = lax.broadcasted_iota(jnp.int32, (c, c), 1)
    tri = ti >= si
    lower = jnp.where(tri, 1.0, 0.0).astype(BF16)
    upper = jnp.where(ti <= si, 1.0, 0.0).astype(BF16)
    cs_col = sum(_dot(lower, part) for part in _split3(dta))
    cs_row = sum(_dot(part, upper) for part in _split3(dta_t))
    ecs = jnp.exp(cs_col)
    cs_last = cs_col[c - 1:c, :]
    tail = jnp.exp(cs_last - cs_col)
    elast = jnp.exp(cs_last)

    ys, hs = [], []
    hpg = SSM_HEADS // SSM_GROUPS
    for g in range(SSM_GROUPS):
        bg = bm[:, g * SSM_N:(g + 1) * SSM_N]
        cgb = cm[:, g * SSM_N:(g + 1) * SSM_N].astype(BF16)
        gmat = _dot_nt(cgb, bg.astype(BF16))
        for hh in range(g * hpg, (g + 1) * hpg):
            seg = cs_col[:, hh:hh + 1] - cs_row[hh:hh + 1, :]
            decay = jnp.exp(jnp.where(tri, seg, NEG))
            xdt = (xs[:, hh * SSM_P:(hh + 1) * SSM_P] * dt[:, hh:hh + 1]).astype(BF16)
            h_h = hst[hh * SSM_P:(hh + 1) * SSM_P, :]
            ys.append(_dot((gmat * decay).astype(BF16), xdt) + _dot_nt(cgb, h_h.astype(BF16)) * ecs[:, hh:hh + 1])
            hs.append(h_h * elast[:, hh:hh + 1] + _dot_tn(xdt, (bg * tail[:, hh:hh + 1]).astype(BF16)))
    y = jnp.concatenate(ys, axis=1) + xs * dsk
    y = y * _silu(z)
    gn = SSM_INNER // SSM_GROUPS
    y = jnp.concatenate([_rms(y[:, g * gn:(g + 1) * gn], gnw[:, g * gn:(g + 1) * gn]) for g in range(SSM_GROUPS)], axis=1)
    return y, jnp.concatenate(hs, axis=0)


def _ssd_prompt_kernel(xbc_ref, prev_ref, z_ref, dt_ref, dtt_ref, cw_ref, cb_ref, dtb_ref, dtbt_ref, alog_ref, alogt_ref,
                       dsk_ref, gnw_ref, y_ref, hl_ref, hs_ref):
    c = pl.program_id(1)

    @pl.when(c == 0)
    def _():
        hs_ref[...] = jnp.zeros(hs_ref.shape, F32)

    prev = prev_ref[...] * (c > 0).astype(F32)
    y, hn = _ssd_chunk(xbc_ref[...], prev, z_ref[...], dt_ref[...], dtt_ref[0], hs_ref[...], cw_ref[...], cb_ref[...],
                       dtb_ref[...], dtbt_ref[...], alog_ref[...], alogt_ref[...], dsk_ref[...], gnw_ref[...])
    y_ref[...] = y.astype(BF16)
    hs_ref[...] = hn
    hl_ref[0] = hn


def _ssd_prompt(xbc, z, dt, dt_t, params, bsz, n):
    c = SSD_CHUNK
    nc = n // c
    cur = lambda width: pl.BlockSpec((c, width), lambda b, i: (b * nc + i, 0))
    full = lambda a: pl.BlockSpec(a.shape, lambda b, i: (0,) * a.ndim)
    hd = SSM_HEADS * SSM_P
    return pl.pallas_call(
        _ssd_prompt_kernel,
        grid=(bsz, nc),
        in_specs=[cur(CONV_DIM), pl.BlockSpec((c, CONV_DIM), lambda b, i: (b * nc + jnp.maximum(i - 1, 0), 0)),
                  cur(SSM_INNER), cur(LANES), pl.BlockSpec((1, SSM_HEADS, c), lambda b, i: (b, 0, i))]
        + [full(p) for p in params],
        out_specs=[cur(SSM_INNER), pl.BlockSpec((1, hd, SSM_N), lambda b, i: (b, 0, 0))],
        out_shape=[jax.ShapeDtypeStruct((bsz * n, SSM_INNER), BF16), jax.ShapeDtypeStruct((bsz, hd, SSM_N), F32)],
        scratch_shapes=[pltpu.VMEM((hd, SSM_N), F32)],
        compiler_params=pltpu.CompilerParams(dimension_semantics=("arbitrary", "arbitrary"), vmem_limit_bytes=VMEM_LIMIT),
        name="ssd_prompt",
    )(xbc, xbc, z, dt, dt_t, *params)


def _ssd_sample_kernel(n_seq, xbc_ref, st_ref, z_ref, dt_ref, dtt_ref, h0_ref, cw_ref, cb_ref, dtb_ref, dtbt_ref,
                       alog_ref, alogt_ref, dsk_ref, gnw_ref, y_ref, hn_ref):
    def one_seq(s, carry):
        y, hn = _ssd_chunk(xbc_ref[s], st_ref[s], z_ref[s], dt_ref[s], dtt_ref[s], h0_ref[s], cw_ref[...], cb_ref[...],
                           dtb_ref[...], dtbt_ref[...], alog_ref[...], alogt_ref[...], dsk_ref[...], gnw_ref[...])
        y_ref[s] = y
        hn_ref[s] = hn
        return carry

    lax.fori_loop(0, n_seq, one_seq, 0)


def _ssd_sample(xbc, st, z, dt, dt_t, h0, params, n_seq):
    sb = xbc.shape[0]
    lead = lambda a: pl.BlockSpec((n_seq,) + a.shape[1:], lambda i: (i, 0, 0))
    full = lambda a: pl.BlockSpec(a.shape, lambda i: (0,) * a.ndim)
    seq = (xbc, st, z, dt, dt_t, h0)
    return pl.pallas_call(
        functools.partial(_ssd_sample_kernel, n_seq),
        grid=(sb // n_seq,),
        in_specs=[lead(a) for a in seq] + [full(p) for p in params],
        out_specs=[lead(z), lead(h0)],
        out_shape=[jax.ShapeDtypeStruct(z.shape, F32), jax.ShapeDtypeStruct(h0.shape, F32)],
        compiler_params=pltpu.CompilerParams(dimension_semantics=("arbitrary",), vmem_limit_bytes=VMEM_LIMIT),
        name="ssd_sample",
    )(*seq, *params)


def _moba_prompt_kernel(q_ref, k_ref, v_ref, km_ref, down_ref, dprev_ref, cfar_ref, o_ref,
                        qa_ref, s_ref, p_ref, m_ref, al_ref, acc_ref):
    cur = pl.program_id(2)
    tq = q_ref.shape[0]
    rows = ATT_GROUP * tq
    qs = jnp.concatenate([q_ref[:, g * HD_C:(g + 1) * HD_C] for g in range(ATT_GROUP)], axis=0)
    lane = lax.broadcasted_iota(jnp.int32, (rows, LANES), 1)
    gate = jnp.where(lane < cur, _dot_nt(qs, km_ref[0, 0]), NEG)
    sel = _top3_mask(gate, lane.astype(F32))
    allowed = jnp.where(lane < cur, sel, jnp.where(lane == cur, 1.0, 0.0))
    c = cfar_ref[0]
    c_hi = c.astype(BF16)
    c_lo = (c - c_hi.astype(F32)).astype(BF16)
    upper = jnp.where(lane == CONST_LANE, c_hi, jnp.where(lane == CONST_LANE + 1, c_lo,
                                                         jnp.where(allowed > 0.5, 0.0, NEG).astype(BF16)))
    qa_ref[:, :HD_C] = qs
    qa_ref[:, HD_C:] = upper
    m_ref[...] = jnp.full(m_ref.shape, LOWEST, F32)
    acc_ref[...] = jnp.zeros(acc_ref.shape, F32)

    groups = [slice(i * rows // MXU_ROW_GROUPS, (i + 1) * rows // MXU_ROW_GROUPS) for i in range(MXU_ROW_GROUPS)]

    def score(j, slot):
        r0 = pl.multiple_of(j * MOBA_BLOCK, MOBA_BLOCK)
        kblk = k_ref[pl.ds(r0, MOBA_BLOCK), :]
        for gs in groups:
            s_ref[slot, gs, :] = _dot_nt(qa_ref[gs, :], kblk)

    def consume(j, slot, delta_ref):
        for r in range(rows // SUB_ROWS):
            sl = slice(r * SUB_ROWS, (r + 1) * SUB_ROWS)
            halves = [s_ref[slot, sl, hf * LANES:(hf + 1) * LANES] for hf in range(2)]
            if delta_ref is not None:
                halves = [sh + delta_ref[0, sl, hf * LANES:(hf + 1) * LANES] for hf, sh in enumerate(halves)]
            m_old = m_ref[sl, :]
            m_new = jnp.maximum(m_old, jnp.max(jnp.maximum(halves[0], halves[1]), axis=-1, keepdims=True))
            for hf, sh in enumerate(halves):
                p_ref[sl, hf * LANES:(hf + 1) * LANES] = jnp.exp2(sh - m_new).astype(BF16)
            al_ref[sl, :] = jnp.exp2(m_old - m_new)
            m_ref[sl, :] = m_new
        r0 = pl.multiple_of(j * MOBA_BLOCK, MOBA_BLOCK)
        vblk = v_ref[pl.ds(r0, MOBA_BLOCK), :]
        for gs in groups:
            acc_ref[gs, :] = jnp.concatenate([al_ref[gs, :]] * 2, axis=1) * acc_ref[gs, :] + _dot(p_ref[gs, :], vblk)

    score(cur, 0)
    consume(cur, 0, down_ref)

    @pl.when(cur >= 1)
    def _():
        score(cur - 1, 0)
        consume(cur - 1, 0, dprev_ref)

    n_far = jnp.maximum(cur - 1, 0)
    last = jnp.maximum(n_far - 1, 0)

    @pl.when(n_far > 0)
    def _():
        score(0, 0)

    def far_pair(t, carry):
        j = 2 * t
        score(jnp.minimum(j + 1, last), 1)
        consume(j, 0, None)

        @pl.when(j + 1 < n_far)
        def _():
            score(jnp.minimum(j + 2, last), 0)
            consume(j + 1, 1, None)

        return carry

    lax.fori_loop(0, (n_far + 1) // 2, far_pair, 0)
    o = acc_ref[:, :HD_C] / acc_ref[:, HD_C:]
    for g in range(ATT_GROUP):
        o_ref[:, g * HD_C:(g + 1) * HD_C] = o[g * tq:(g + 1) * tq].astype(BF16)


def _moba_prompt(q, kaug, vaug, kmean, down, dprev, cfar, bsz, n):
    tq = MOBA_BLOCK
    nq = n // tq
    assert nq <= CONST_LANE
    rows = ATT_GROUP * tq
    gw = ATT_GROUP * HD_C
    qspec = pl.BlockSpec((tq, gw), lambda b, kv, i: (b * nq + i, kv))
    kvspec = pl.BlockSpec((n, 2 * HD_C), lambda b, kv, i: (b, kv))
    tab = lambda a: pl.BlockSpec((1,) + a.shape[1:], lambda b, kv, i: (kv, 0, 0))
    return pl.pallas_call(
        _moba_prompt_kernel,
        grid=(bsz, ATT_KV, nq),
        in_specs=[qspec, kvspec, kvspec, pl.BlockSpec((1, 1, LANES, HD_C), lambda b, kv, i: (b, kv, 0, 0)),
                  tab(down), tab(dprev), tab(cfar)],
        out_specs=qspec,
        out_shape=jax.ShapeDtypeStruct((bsz * n, C_Q), BF16),
        scratch_shapes=[pltpu.VMEM((rows, 2 * HD_C), BF16), pltpu.VMEM((2, rows, MOBA_BLOCK), F32),
                        pltpu.VMEM((rows, MOBA_BLOCK), BF16), pltpu.VMEM((rows, LANES), F32),
                        pltpu.VMEM((rows, LANES), F32), pltpu.VMEM((rows, 2 * HD_C), F32)],
        compiler_params=pltpu.CompilerParams(dimension_semantics=("arbitrary",) * 3, vmem_limit_bytes=VMEM_LIMIT),
        name="moba_prompt",
    )(q, kaug, vaug, kmean, down, dprev, cfar)


def _moba_sample_kernel(nbs, ppb, pt_ref, q_ref, *refs):
    npg = nbs * ppb
    k_pages, v_pages = refs[:npg], refs[npg:2 * npg]
    kn_ref, vn_ref, tbl_ref, bown_ref, o_ref, gate_sc, m_sc, l_sc, o_sc = refs[2 * npg:]
    step = pl.program_id(1)
    nblk = pl.num_programs(1) * nbs
    rows = q_ref.shape[1]
    lane = lax.broadcasted_iota(jnp.int32, (rows, LANES), 1)

    @pl.when(step == 0)
    def _():
        gate_sc[...] = jnp.zeros(gate_sc.shape, F32)
        m_sc[...] = jnp.zeros(m_sc.shape, F32)
        l_sc[...] = jnp.zeros(l_sc.shape, F32)

    q = q_ref[0]
    half = rows // ATT_KV
    lane_h = lax.broadcasted_iota(jnp.int32, (half, LANES), 1)
    own = lambda pages, kv: jnp.concatenate([r[pl.ds(kv, PAGE_SIZE, stride=ATT_KV), :] for r in pages], axis=0)
    keys = [own(k_pages, kv) for kv in range(ATT_KV)]
    scores = [_dot_nt(q[kv * half:(kv + 1) * half], keys[kv].astype(BF16)) for kv in range(ATT_KV)]
    for kv in range(ATT_KV):
        rs = slice(kv * half, (kv + 1) * half)
        kk, s_all = keys[kv], scores[kv]
        vv = own(v_pages, kv)
        qf = q[rs].astype(F32)
        gate_n, m_n, l_n = gate_sc[rs, :], m_sc[rs, :], l_sc[rs, :]
        for i in range(nbs):
            j = step * nbs + i
            cs = slice(i * MOBA_BLOCK, (i + 1) * MOBA_BLOCK)
            kmean = (jnp.sum(kk[cs], axis=0, keepdims=True) * (1.0 / MOBA_BLOCK)).astype(BF16).astype(F32)
            gate = jnp.sum(qf * kmean, axis=-1, keepdims=True)
            s = s_all[:, cs] + tbl_ref[jnp.where(j == nblk - 1, 1, 0), rs, :]
            m = jnp.max(s, axis=-1, keepdims=True)
            p = jnp.exp2(s - m)
            hit = lane_h == j
            gate_n = jnp.where(hit, gate, gate_n)
            m_n = jnp.where(hit, m, m_n)
            l_n = jnp.where(hit, jnp.sum(p, axis=-1, keepdims=True), l_n)
            o_sc[j, rs, :] = _dot(p.astype(BF16), vv[cs].astype(BF16))
        gate_sc[rs, :], m_sc[rs, :], l_sc[rs, :] = gate_n, m_n, l_n

    @pl.when(step == pl.num_programs(1) - 1)
    def _():
        pad = jnp.zeros((LANES - kn_ref.shape[1], HD_C), F32)
        sel = _top3_mask(jnp.where(lane < nblk, gate_sc[...], LOWEST), lane.astype(F32)) > 0.5
        m_blk = jnp.where(sel, m_sc[...], LOWEST)
        kn = jnp.concatenate([kn_ref[0], pad], axis=0).astype(BF16)
        vn = jnp.concatenate([vn_ref[0], pad], axis=0).astype(BF16)
        s_own = _dot_nt(q, kn) + bown_ref[...]
        m_tot = jnp.maximum(jnp.max(m_blk, axis=-1, keepdims=True), jnp.max(s_own, axis=-1, keepdims=True))
        w = jnp.exp2(m_blk - m_tot)
        p_own = jnp.exp2(s_own - m_tot)
        den = jnp.sum(w * l_sc[...], axis=-1, keepdims=True) + jnp.sum(p_own, axis=-1, keepdims=True)
        o = _dot(p_own.astype(BF16), vn)
        for b in range(o_sc.shape[0]):
            o = o + w[:, b:b + 1] * o_sc[b]
        o_ref[0] = o / den


def _moba_sample(pt, q, pool_k, pool_v, kn, vn, tbl, bown, n_pages):
    sb, rows, _ = q.shape
    ppb = MOBA_BLOCK // PAGE_SIZE
    nblk = n_pages // ppb
    nbs = SAMPLE_BLOCKS_PER_STEP
    assert nblk % nbs == 0 and nblk <= LANES

    def page(p):
        return pl.BlockSpec((PAGE_SIZE * ATT_KV, HD_C), lambda b, s, pt_ref: (pt_ref[b * n_pages + s * nbs * ppb + p], 0))

    seq = lambda a: pl.BlockSpec((1,) + a.shape[1:], lambda b, s, pt_ref: (b,) + (0,) * (a.ndim - 1))
    full = lambda a: pl.BlockSpec(a.shape, lambda b, s, pt_ref: (0,) * a.ndim)
    pages = [page(p) for p in range(nbs * ppb)]
    grid_spec = pltpu.PrefetchScalarGridSpec(
        num_scalar_prefetch=1,
        grid=(sb, nblk // nbs),
        in_specs=[seq(q)] + pages * 2 + [seq(kn), seq(vn), full(tbl), full(bown)],
        out_specs=seq(q),
        scratch_shapes=[pltpu.VMEM((rows, LANES), F32)] * 3 + [pltpu.VMEM((nblk, rows, HD_C), F32)],
    )
    return pl.pallas_call(
        functools.partial(_moba_sample_kernel, nbs, ppb),
        grid_spec=grid_spec,
        out_shape=jax.ShapeDtypeStruct(q.shape, F32),
        compiler_params=pltpu.CompilerParams(dimension_semantics=("arbitrary", "arbitrary"), vmem_limit_bytes=VMEM_LIMIT),
        name="moba_sample",
    )(pt, q, *([pool_k] * len(pages)), *([pool_v] * len(pages)), kn, vn, tbl, bown)


def _bias_table(tab_t, head, dist, valid):
    rows_tab = jnp.take(tab_t, jnp.asarray(np.asarray(head, np.int32)), axis=0)
    bucket = jnp.asarray(_bucket_np(np.broadcast_to(dist, np.broadcast_shapes(np.shape(dist), np.shape(valid)))))
    out = jnp.zeros(bucket.shape, F32)
    for b in range(N_BUCKETS):
        out = jnp.where(bucket == b, rows_tab[:, b:b + 1], out)
    return jnp.where(jnp.asarray(np.broadcast_to(valid, bucket.shape)), out, NEG)


def _head_rows(per_head):
    return np.repeat(np.arange(ATT_HEADS), per_head)


def _to_group_rows(a, sb, sn, hd):
    return a.reshape(sb, sn, ATT_KV, ATT_GROUP, hd).transpose(0, 2, 3, 1, 4).reshape(sb, ATT_KV, ATT_GROUP * sn, hd)


def _from_group_rows(a, sb, sn, hd):
    return a.reshape(sb, ATT_KV, ATT_GROUP, sn, hd).transpose(0, 3, 1, 2, 4).reshape(sb * sn, ATT_HEADS * hd)


def _pad_lanes(v):
    return jnp.pad(v.astype(F32), (0, LANES - v.shape[0]))[None]


def kernel(x_prompt, x_sample, cache_win_k, cache_win_v, state_conv, state_ssm, cache_moba_k, cache_moba_v, cache_mem_k, cache_mem_v, page_table, mem_prompt, rel_bias, norm_mix, norm_mem, norm_ffn, even_w_in, even_w_out, even_q_norm, even_k_norm, even_sinks, even_conv_w, even_conv_b, even_dt_bias, even_a_log, even_d_skip, even_gate_norm, odd_w_in, odd_w_out, odd_q_norm, odd_k_norm, mem_wq, mem_wk, mem_wv, mem_wo, mem_q_norm, mem_k_norm, ffn_w_gate, ffn_w_up, ffn_w_down):
    bsz, n = x_prompt.shape[:2]
    sb, sn = x_sample.shape[:2]
    depth = norm_mix.shape[0]
    rp, rs = bsz * n, sb * sn
    n_pages = page_table.shape[1]
    n_pool = cache_moba_k.shape[1]
    past_len = n_pages * PAGE_SIZE
    assert past_len % MOBA_BLOCK == 0 and cache_win_k.shape[2] == WINDOW and sn >= CONV_W - 1
    seq_group = 16

    x = jnp.concatenate([x_prompt.reshape(rp, D_MODEL), x_sample.reshape(rs, D_MODEL)], axis=0)
    tab_t = rel_bias.T.astype(F32)

    qi = np.tile(np.arange(WINDOW), ATT_HEADS)[:, None]
    kj = np.arange(2 * WINDOW)[None, :]
    d = WINDOW + qi - kj
    swa_bias_p = _bias_table(tab_t, _head_rows(WINDOW), d, (d >= 0) & (d < WINDOW)).reshape(ATT_HEADS, WINDOW, 2 * WINDOW)
    ti = np.tile(np.arange(sn), ATT_HEADS)[:, None]
    d = np.where(kj < WINDOW, WINDOW + ti - kj, ti - (kj - WINDOW))
    swa_bias_s = _bias_table(tab_t, _head_rows(sn), d, (d >= 0) & (d < WINDOW) & (kj < WINDOW + sn))
    assert _bucket_np(MOBA_BLOCK + 1) == N_BUCKETS - 1
    far_bias = tab_t[:, N_BUCKETS - 1] * LOG2E
    qi = np.tile(np.arange(MOBA_BLOCK), ATT_HEADS)[:, None]
    kj = np.arange(MOBA_BLOCK)[None, :]
    heads_p = _head_rows(MOBA_BLOCK)
    cfar_p = jnp.take(far_bias, jnp.asarray(heads_p))[:, None]
    grp = lambda t: t.reshape(ATT_KV, ATT_GROUP * MOBA_BLOCK, t.shape[-1])
    down_p = grp(jnp.where(jnp.asarray(qi >= kj), _bias_table(tab_t, heads_p, qi - kj, True) * LOG2E - cfar_p, NEG))
    dprev_p = grp(_bias_table(tab_t, heads_p, MOBA_BLOCK + qi - kj, True) * LOG2E - cfar_p)
    cfar_p = grp(jnp.broadcast_to(cfar_p, (cfar_p.shape[0], LANES)))
    heads_s = _head_rows(sn)
    row_kv = (heads_s // ATT_GROUP)[:, None]
    cfar_s = jnp.take(far_bias, jnp.asarray(heads_s))[:, None]
    tbl_s = jnp.stack([jnp.broadcast_to(cfar_s, (cfar_s.shape[0], MOBA_BLOCK)),
                       _bias_table(tab_t, heads_s, MOBA_BLOCK + ti - kj, True) * LOG2E])
    col = np.arange(LANES)[None, :]
    tok = col // ATT_KV
    bown_s = _bias_table(tab_t, heads_s, ti - tok, (row_kv == col % ATT_KV) & (ti >= tok) & (tok < sn)) * LOG2E

    pool_k = cache_moba_k.reshape(-1, HD_C)
    pool_v = cache_moba_v.reshape(-1, HD_C)
    pt_flat = page_table.reshape(-1)
    mem_k = cache_mem_k.reshape(depth * sb, MEM_LEN, MEM_W)
    mem_v = cache_mem_v.reshape(depth * sb, MEM_LEN, MEM_W)
    mem_rows = mem_prompt.reshape(bsz * MEM_LEN, D_MODEL)

    pwk, pwv, pcv, pss, pmk, pmv, pxk, pxv = [], [], [], [], [], [], [], []
    swk, swv, scv, sss, smk, smv = [], [], [], [], [], []
    for l in range(depth):
        g_mix = norm_mix[l][None]
        if l % 2 == 0:
            e = l // 2
            w_in = jnp.pad(even_w_in[e], ((0, 0), (0, EVEN_W - even_w_in.shape[2]))).astype(BF16)
            w_out = even_w_out[e].astype(BF16)
            qw = jnp.tile(even_q_norm[e], ATT_HEADS)[None]
            kw = jnp.tile(even_k_norm[e], ATT_KV)[None]
            q, k, v, z, xbc, dt = _proj_even(x, g_mix, w_in, qw, kw)
            params = (even_conv_w[e], even_conv_b[e][None], _pad_lanes(even_dt_bias[e]), even_dt_bias[e][:, None],
                      _pad_lanes(even_a_log[e]), even_a_log[e][:, None], jnp.repeat(even_d_skip[e], SSM_P)[None],
                      even_gate_norm[e][None])
            att_p = _swa_prompt(even_sinks[e], q, k, v, swa_bias_p, bsz, n)
            dt_t = dt[:rp, :SSM_HEADS].reshape(bsz, n, SSM_HEADS).transpose(0, 2, 1)
            y_p, h_p = _ssd_prompt(xbc, z, dt, dt_t, params, bsz, n)
            sink_col = jnp.repeat(even_sinks[e], sn)[:, None]
            k_s = k[rp:].reshape(sb, sn, E_KV)
            v_s = v[rp:].reshape(sb, sn, E_KV)
            qg = _to_group_rows(q[rp:], sb, sn, HD_A)
            zq = jnp.zeros_like(qg[:, 0])
            q_bd = jnp.concatenate([jnp.concatenate([qg[:, 0], zq], axis=-1),
                                    jnp.concatenate([zq, qg[:, 1]], axis=-1)], axis=1)
            att_s = _swa_sample(q_bd, cache_win_k[e].reshape(sb, WINDOW, E_KV), cache_win_v[e].reshape(sb, WINDOW, E_KV),
                                k_s, v_s, swa_bias_s, sink_col, seq_group)
            xbc_s = xbc[rp:].reshape(sb, sn, CONV_DIM)
            st = jnp.pad(state_conv[e], ((0, 0), (sn - (CONV_W - 1), 0), (0, 0)))
            dt_s = dt[rp:].reshape(sb, sn, LANES)
            y_s, h_s = _ssd_sample(xbc_s, st, z[rp:].reshape(sb, sn, SSM_INNER), dt_s,
                                   dt_s[:, :, :SSM_HEADS].transpose(0, 2, 1),
                                   state_ssm[e].reshape(sb, SSM_HEADS * SSM_P, SSM_N), params, seq_group)
            att = jnp.concatenate([att_p, _from_group_rows(att_s, sb, sn, HD_A).astype(BF16)], axis=0)
            yy = jnp.concatenate([y_p, y_s.reshape(rs, SSM_INNER).astype(BF16)], axis=0)
            x = _outproj(x, [att, yy], [w_out[:E_Q], w_out[E_Q:]])
            tail = lambda a, cnt: jnp.stack([a[(b + 1) * n - cnt:(b + 1) * n] for b in range(bsz)])
            pwk.append(tail(k, WINDOW).reshape(bsz, WINDOW, ATT_KV, HD_A))
            pwv.append(tail(v, WINDOW).reshape(bsz, WINDOW, ATT_KV, HD_A))
            pcv.append(tail(xbc, CONV_W - 1))
            pss.append(h_p.reshape(bsz, SSM_HEADS, SSM_P, SSM_N))
            swk.append(jnp.concatenate([cache_win_k[e], k_s.reshape(sb, sn, ATT_KV, HD_A)], axis=1)[:, -WINDOW:])
            swv.append(jnp.concatenate([cache_win_v[e], v_s.reshape(sb, sn, ATT_KV, HD_A)], axis=1)[:, -WINDOW:])
            scv.append(jnp.concatenate([state_conv[e], xbc_s], axis=1)[:, -(CONV_W - 1):])
            sss.append(h_s.reshape(sb, SSM_HEADS, SSM_P, SSM_N))
        else:
            o = l // 2
            w_in = odd_w_in[o].astype(BF16)
            w_out = odd_w_out[o].astype(BF16)
            nb = n // MOBA_BLOCK
            q, k, v, kaug, vaug, km = _proj_odd(x, g_mix, w_in, odd_q_norm[o][None], odd_k_norm[o][None], nb)
            kmean = km[:bsz * nb, 0].reshape(bsz, nb, ATT_KV, HD_C).transpose(0, 2, 1, 3)
            kmean = jnp.pad(kmean, ((0, 0), (0, 0), (0, LANES - nb), (0, 0))).astype(BF16)
            att_p = _moba_prompt(q, kaug, vaug, kmean, down_p, dprev_p, cfar_p, bsz, n)
            k_s = k[rp:].reshape(sb, sn, C_KV)
            v_s = v[rp:].reshape(sb, sn, C_KV)
            q_s = _to_group_rows(q[rp:], sb, sn, HD_C).reshape(sb, ATT_HEADS * sn, HD_C)
            att_s = _moba_sample(pt_flat + o * n_pool, q_s, pool_k, pool_v, k_s.reshape(sb, sn * ATT_KV, HD_C),
                                 v_s.reshape(sb, sn * ATT_KV, HD_C), tbl_s, bown_s, n_pages)
            att = jnp.concatenate([att_p, _from_group_rows(att_s, sb, sn, HD_C).astype(BF16)], axis=0)
            x = _outproj(x, [att], [w_out])
            pmk.append(k[:rp].reshape(bsz, n, ATT_KV, HD_C))
            pmv.append(v[:rp].reshape(bsz, n, ATT_KV, HD_C))
            smk.append(k_s.reshape(sb, sn, ATT_KV, HD_C))
            smv.append(v_s.reshape(sb, sn, ATT_KV, HD_C))
        mqw = jnp.tile(mem_q_norm[l], MEM_HEADS)[None]
        mkw = jnp.tile(mem_k_norm[l], MEM_HEADS)[None]
        wq = mem_wq[l].astype(BF16)
        wo = mem_wo[l].astype(BF16)
        xk, xv = _memkv(mem_rows, mem_wk[l].astype(BF16), mem_wv[l].astype(BF16), mkw)
        pxk.append(xk.reshape(bsz, MEM_LEN, MEM_HEADS, MEM_HD))
        pxv.append(xv.reshape(bsz, MEM_LEN, MEM_HEADS, MEM_HD))
        g_mem = norm_mem[l][None]
        x = _memattn(x, g_mem, wq, wo, mqw, xk.reshape(bsz, MEM_LEN, MEM_W), xv.reshape(bsz, MEM_LEN, MEM_W),
                     0, bsz, 0, 1, ROW_TILE, n // ROW_TILE)
        x = _memattn(x, g_mem, wq, wo, mqw, mem_k, mem_v, rp, sb, l * sb, seq_group, sn, 1)
        x = _ffn(x, norm_ffn[l][None], ffn_w_gate[l].astype(BF16), ffn_w_up[l].astype(BF16), ffn_w_down[l].astype(BF16))

    return (x[:rp].reshape(bsz, n, D_MODEL), x[rp:].reshape(sb, sn, D_MODEL),
            jnp.stack(pwk), jnp.stack(pwv), jnp.stack(pcv), jnp.stack(pss), jnp.stack(pmk), jnp.stack(pmv),
            jnp.stack(pxk), jnp.stack(pxv), jnp.stack(swk), jnp.stack(swv), jnp.stack(scv), jnp.stack(sss),
            jnp.stack(smk), jnp.stack(smv))
```

```python
import functools
import math

import jax
import jax.numpy as jnp
import numpy as np
from jax import lax
from jax.experimental import pallas as pl
from jax.experimental.pallas import tpu as pltpu

F32 = jnp.float32
BF16 = jnp.bfloat16

D_MODEL = 1024
ATT_HEADS = 8
ATT_KV = 2
ATT_GROUP = ATT_HEADS // ATT_KV
N_BUCKETS = 32
MAX_DIST = 128
WINDOW = 128
HD_A = 64
SSM_HEADS = 8
SSM_P = 64
SSM_INNER = SSM_HEADS * SSM_P
SSM_GROUPS = 2
SSM_N = 64
CONV_W = 4
CONV_DIM = SSM_INNER + 2 * SSM_GROUPS * SSM_N
SSD_CHUNK = 128
HD_C = 128
MOBA_BLOCK = 256
MOBA_TOPK = 3
PAGE_SIZE = 128
MEM_LEN = 256
MEM_HEADS = 4
MEM_HD = 64
MEM_W = MEM_HEADS * MEM_HD
E_Q = ATT_HEADS * HD_A
E_KV = ATT_KV * HD_A
C_Q = ATT_HEADS * HD_C
C_KV = ATT_KV * HD_C
EPS = 1e-6
NEG = -1e30
LOWEST = -3e38

LOG2E = 1.4426950408889634
LANES = 128
CONST_LANE = LANES - 2
SUB_ROWS = 128
MXU_ROW_GROUPS = 2
SAMPLE_BLOCKS_PER_STEP = 8
SCORE_BLOCKS = 4
SEQ_UNROLL = 4
EVEN_W = E_Q + 2 * E_KV + SSM_INNER + CONV_DIM + LANES
ROW_TILE = 512
FFN_ROW_TILE = 1024
FFN_COL_TILE = 1408
VMEM_LIMIT = 56 * 1024 * 1024


def _dot(a, b):
    return jnp.dot(a, b, preferred_element_type=F32)


def _dot_nt(a, b):
    return lax.dot_general(a, b, (((1,), (1,)), ((), ())), preferred_element_type=F32)


def _dot_tn(a, b):
    return lax.dot_general(a, b, (((0,), (0,)), ((), ())), preferred_element_type=F32)


def _split3(x):
    hi = x.astype(BF16)
    r = x - hi.astype(F32)
    mid = r.astype(BF16)
    lo = (r - mid.astype(F32)).astype(BF16)
    return hi, mid, lo


def _rms(x, w):
    ms = jnp.mean(x * x, axis=-1, keepdims=True)
    return x * lax.rsqrt(ms + EPS) * w


def _headnorm_bd(y, ones_bd, w, hd):
    sq = y * y
    hi = sq.astype(BF16)
    lo = (sq - hi.astype(F32)).astype(BF16)
    ss = _dot(hi, ones_bd) + _dot(lo, ones_bd)
    return y * lax.rsqrt(ss * (1.0 / hd) + EPS) * w


def _silu(x):
    return x * jax.nn.sigmoid(x)


def _softplus(x):
    return jnp.maximum(x, 0.0) + jnp.log1p(jnp.exp(-jnp.abs(x)))


def _bucket_np(dist):
    n = np.maximum(np.asarray(dist, np.int64), 0)
    nf = np.maximum(n, 1).astype(np.float64)
    exact = N_BUCKETS // 2
    large = exact + np.floor(np.log(nf / exact) / math.log(MAX_DIST / exact) * (N_BUCKETS - exact) + 1e-6).astype(np.int64)
    return np.where(n < exact, n, np.minimum(large, N_BUCKETS - 1)).astype(np.int32)


def _block_diag_ones(width, hd):
    i = np.arange(width) // hd
    return jnp.asarray((i[:, None] == i[None, :]).astype(np.float32), BF16)


def _top3_mask(gate, lane_f):
    sel = jnp.zeros_like(gate)
    for _ in range(MOBA_TOPK):
        mx = jnp.max(gate, axis=-1, keepdims=True)
        idx = jnp.min(jnp.where(gate == mx, lane_f, 1e9), axis=-1, keepdims=True)
        hit = lane_f == idx
        sel = jnp.where(hit, 1.0, sel)
        gate = jnp.where(hit, LOWEST, gate)
    return sel


def _proj_even_kernel(x_ref, g_ref, w_ref, bdq_ref, bdk_ref, qw_ref, kw_ref,
                      q_ref, k_ref, v_ref, z_ref, xbc_ref, dt_ref):
    h = _rms(x_ref[...], g_ref[...]).astype(BF16)
    y = _dot(h, w_ref[...])
    q = _headnorm_bd(y[:, :E_Q], bdq_ref[...], qw_ref[...], HD_A)
    q_ref[...] = (q * HD_A ** -0.5).astype(BF16)
    o = E_Q
    k_ref[...] = _headnorm_bd(y[:, o:o + E_KV], bdk_ref[...], kw_ref[...], HD_A)
    o += E_KV
    v_ref[...] = y[:, o:o + E_KV]
    o += E_KV
    z_ref[...] = y[:, o:o + SSM_INNER]
    o += SSM_INNER
    xbc_ref[...] = y[:, o:o + CONV_DIM]
    o += CONV_DIM
    dt_ref[...] = y[:, o:o + LANES]


def _proj_even(x, g, w, qw, kw):
    rows = x.shape[0]
    tm = ROW_TILE
    row = lambda width: pl.BlockSpec((tm, width), lambda i: (i, 0))
    full = lambda a: pl.BlockSpec(a.shape, lambda i: (0,) * a.ndim)
    bdq = _block_diag_ones(E_Q, HD_A)
    bdk = _block_diag_ones(E_KV, HD_A)
    ins = (x, g, w, bdq, bdk, qw, kw)
    widths = (E_Q, E_KV, E_KV, SSM_INNER, CONV_DIM, LANES)
    dts = (BF16, F32, F32, F32, F32, F32)
    return pl.pallas_call(
        _proj_even_kernel,
        grid=(rows // tm,),
        in_specs=[row(D_MODEL)] + [full(a) for a in ins[1:]],
        out_specs=[row(wd) for wd in widths],
        out_shape=[jax.ShapeDtypeStruct((rows, wd), dt) for wd, dt in zip(widths, dts)],
        compiler_params=pltpu.CompilerParams(dimension_semantics=("arbitrary",), vmem_limit_bytes=VMEM_LIMIT),
        name="proj_even",
    )(*ins)


def _proj_odd_kernel(seq_blocks, x_ref, g_ref, w_ref, qw_ref, kw_ref,
                     q_ref, k_ref, v_ref, kaug_ref, vaug_ref, km_ref):
    i = pl.program_id(0)
    tm = x_ref.shape[0]
    h = _rms(x_ref[...], g_ref[...]).astype(BF16)
    y = _dot(h, w_ref[...])
    qw = qw_ref[...]
    for hh in range(ATT_HEADS):
        sl = slice(hh * HD_C, (hh + 1) * HD_C)
        q_ref[:, sl] = (_rms(y[:, sl], qw) * (HD_C ** -0.5 * LOG2E)).astype(BF16)
    nblk = tm // MOBA_BLOCK
    row = lax.broadcasted_iota(jnp.int32, (tm, LANES), 0)
    lane = lax.broadcasted_iota(jnp.int32, (tm, LANES), 1)
    blk = (i * nblk) % seq_blocks + sum(jnp.where(row >= b * MOBA_BLOCK, 1, 0) for b in range(1, nblk))
    onehot = jnp.where(jnp.logical_or(lane == blk, lane >= CONST_LANE), 1.0, 0.0).astype(BF16)
    ones = jnp.ones((tm, LANES), BF16)
    kw = kw_ref[...]
    for kv in range(ATT_KV):
        sl = slice(kv * HD_C, (kv + 1) * HD_C)
        kh = _rms(y[:, C_Q + kv * HD_C:C_Q + (kv + 1) * HD_C], kw)
        vh = y[:, C_Q + C_KV + kv * HD_C:C_Q + C_KV + (kv + 1) * HD_C]
        k_ref[:, sl] = kh
        v_ref[:, sl] = vh
        a = 2 * kv * HD_C
        kaug_ref[:, a:a + HD_C] = kh.astype(BF16)
        kaug_ref[:, a + HD_C:a + 2 * HD_C] = onehot
        vaug_ref[:, a:a + HD_C] = vh.astype(BF16)
        vaug_ref[:, a + HD_C:a + 2 * HD_C] = ones
        for b in range(nblk):
            km_ref[b, :, sl] = jnp.mean(kh[b * MOBA_BLOCK:(b + 1) * MOBA_BLOCK], axis=0, keepdims=True)


def _proj_odd(x, g, w, qw, kw, seq_blocks):
    rows = x.shape[0]
    tm = ROW_TILE
    nblk = tm // MOBA_BLOCK
    row = lambda width: pl.BlockSpec((tm, width), lambda i: (i, 0))
    full = lambda a: pl.BlockSpec(a.shape, lambda i: (0,) * a.ndim)
    ins = (x, g, w, qw, kw)
    widths = (C_Q, C_KV, C_KV, 2 * C_KV, 2 * C_KV)
    dts = (BF16, F32, F32, BF16, BF16)
    return pl.pallas_call(
        functools.partial(_proj_odd_kernel, seq_blocks),
        grid=(rows // tm,),
        in_specs=[row(D_MODEL)] + [full(a) for a in ins[1:]],
        out_specs=[row(wd) for wd in widths] + [pl.BlockSpec((nblk, 1, C_KV), lambda i: (i, 0, 0))],
        out_shape=[jax.ShapeDtypeStruct((rows, wd), dt) for wd, dt in zip(widths, dts)]
        + [jax.ShapeDtypeStruct((rows // MOBA_BLOCK, 1, C_KV), F32)],
        compiler_params=pltpu.CompilerParams(dimension_semantics=("arbitrary",), vmem_limit_bytes=VMEM_LIMIT),
        name="proj_odd",
    )(*ins)


def _outproj_kernel(n_in, x_ref, *refs):
    o_ref = refs[-1]
    acc = x_ref[...]
    for a_ref, w_ref in zip(refs[:n_in], refs[n_in:2 * n_in]):
        acc = acc + _dot(a_ref[...], w_ref[...])
    o_ref[...] = acc


def _outproj(x, acts, ws):
    rows = x.shape[0]
    tm = ROW_TILE
    row = lambda width: pl.BlockSpec((tm, width), lambda i: (i, 0))
    full = lambda a: pl.BlockSpec(a.shape, lambda i: (0,) * a.ndim)
    return pl.pallas_call(
        functools.partial(_outproj_kernel, len(acts)),
        grid=(rows // tm,),
        in_specs=[row(D_MODEL)] + [row(a.shape[1]) for a in acts] + [full(w) for w in ws],
        out_specs=row(D_MODEL),
        out_shape=jax.ShapeDtypeStruct((rows, D_MODEL), F32),
        compiler_params=pltpu.CompilerParams(dimension_semantics=("arbitrary",), vmem_limit_bytes=VMEM_LIMIT),
        name="outproj",
    )(x, *acts, *ws)


def _ffn_kernel(x_ref, g_ref, wg_ref, wu_ref, wd_ref, o_ref, hn_ref, acc_ref):
    j = pl.program_id(1)

    @pl.when(j == 0)
    def _():
        x = x_ref[...]
        hn_ref[...] = _rms(x, g_ref[...]).astype(BF16)
        acc_ref[...] = x

    h = hn_ref[...]
    a = (_silu(_dot(h, wg_ref[...])) * _dot(h, wu_ref[...])).astype(BF16)
    acc_ref[...] += _dot(a, wd_ref[...])

    @pl.when(j == pl.num_programs(1) - 1)
    def _():
        o_ref[...] = acc_ref[...]


def _ffn(x, g, wg, wu, wd):
    rows = x.shape[0]
    d_ff = wg.shape[1]
    tm = FFN_ROW_TILE if rows % FFN_ROW_TILE == 0 else ROW_TILE
    tf = FFN_COL_TILE
    return pl.pallas_call(
        _ffn_kernel,
        grid=(rows // tm, d_ff // tf),
        in_specs=[
            pl.BlockSpec((tm, D_MODEL), lambda i, j: (i, 0)),
            pl.BlockSpec((1, D_MODEL), lambda i, j: (0, 0)),
            pl.BlockSpec((D_MODEL, tf), lambda i, j: (0, j)),
            pl.BlockSpec((D_MODEL, tf), lambda i, j: (0, j)),
            pl.BlockSpec((tf, D_MODEL), lambda i, j: (j, 0)),
        ],
        out_specs=pl.BlockSpec((tm, D_MODEL), lambda i, j: (i, 0)),
        out_shape=jax.ShapeDtypeStruct((rows, D_MODEL), F32),
        scratch_shapes=[pltpu.VMEM((tm, D_MODEL), BF16), pltpu.VMEM((tm, D_MODEL), F32)],
        compiler_params=pltpu.CompilerParams(dimension_semantics=("arbitrary", "arbitrary"), vmem_limit_bytes=VMEM_LIMIT),
        name="ffn",
    )(x, g, wg, wu, wd)


def _memkv_kernel(m_ref, wk_ref, wv_ref, bd_ref, kw_ref, k_ref, v_ref):
    mb = m_ref[...].astype(BF16)
    k_ref[...] = _headnorm_bd(_dot(mb, wk_ref[...]), bd_ref[...], kw_ref[...], MEM_HD)
    v_ref[...] = _dot(mb, wv_ref[...])


def _memkv(mem, wk, wv, kw):
    rows = mem.shape[0]
    bd = _block_diag_ones(MEM_W, MEM_HD)
    return pl.pallas_call(
        _memkv_kernel,
        out_shape=[jax.ShapeDtypeStruct((rows, MEM_W), F32)] * 2,
        compiler_params=pltpu.CompilerParams(vmem_limit_bytes=VMEM_LIMIT),
        name="memkv",
    )(mem, wk, wv, bd, kw)


def _memattn_kernel(n_seq, n, x_ref, g_ref, wq_ref, wo_ref, bd_ref, qw_ref, mk_ref, mv_ref, o_ref, q_sc, a_sc):
    x = x_ref[...]
    h = _rms(x, g_ref[...]).astype(BF16)
    q = _headnorm_bd(_dot(h, wq_ref[...]), bd_ref[...], qw_ref[...], MEM_HD)
    q_sc[...] = q * MEM_HD ** -0.5

    def softmax(sc):
        p = jnp.exp(sc - jnp.max(sc, axis=-1, keepdims=True))
        return (p / jnp.sum(p, axis=-1, keepdims=True)).astype(BF16)

    def one_seq(s, carry):
        r0 = pl.multiple_of(s * n, n)
        qs = q_sc[pl.ds(r0, n), :].astype(BF16)
        mk = mk_ref[s].astype(BF16)
        mv = mv_ref[s].astype(BF16)
        outs = []
        for hh in range(MEM_HEADS):
            sl = slice(hh * MEM_HD, (hh + 1) * MEM_HD)
            outs.append(_dot(softmax(_dot_nt(qs[:, sl], mk[:, sl])), mv[:, sl]))
        a_sc[pl.ds(r0, n), :] = jnp.concatenate(outs, axis=1)
        return carry

    def one_short_seq(own, s, carry):
        r0 = pl.multiple_of(s * n, n)
        qs = jnp.where(own, jnp.concatenate([q_sc[pl.ds(r0, n), :]] * MEM_HEADS, axis=0), 0.0).astype(BF16)
        pv = _dot_nt(softmax(_dot(qs, mk_ref[s].astype(BF16))), mv_ref[s].astype(BF16))
        pv = jnp.where(own, pv, 0.0)
        a_sc[pl.ds(r0, n), :] = sum(pv[hh * n:(hh + 1) * n] for hh in range(MEM_HEADS))
        return carry

    if n_seq > 1:
        own = (lax.broadcasted_iota(jnp.int32, (MEM_HEADS * n, MEM_W), 0) // n
               == lax.broadcasted_iota(jnp.int32, (MEM_HEADS * n, MEM_W), 1) // MEM_HD)
        lax.fori_loop(0, n_seq, functools.partial(one_short_seq, own), 0, unroll=SEQ_UNROLL)
    else:
        one_seq(0, 0)
    o_ref[...] = x + _dot(a_sc[...].astype(BF16), wo_ref[...])


def _memattn(x, g, wq, wo, qw, mk, mv, row0, n_mem, mem0, n_seq, n, steps_per_mem):
    tm = n_seq * n
    steps = n_mem // n_seq * steps_per_mem
    blk0 = row0 // tm
    memblk0 = mem0 // n_seq
    bd = _block_diag_ones(MEM_W, MEM_HD)
    full = lambda a: pl.BlockSpec(a.shape, lambda i: (0,) * a.ndim)
    row_spec = pl.BlockSpec((tm, D_MODEL), lambda i: (blk0 + i, 0))
    mem_spec = pl.BlockSpec((n_seq, MEM_LEN, MEM_W), lambda i: (memblk0 + i // steps_per_mem, 0, 0))
    return pl.pallas_call(
        functools.partial(_memattn_kernel, n_seq, n),
        grid=(steps,),
        in_specs=[row_spec, full(g), full(wq), full(wo), full(bd), full(qw), mem_spec, mem_spec],
        out_specs=row_spec,
        out_shape=jax.ShapeDtypeStruct(x.shape, F32),
        input_output_aliases={0: 0},
        scratch_shapes=[pltpu.VMEM((tm, MEM_W), F32), pltpu.VMEM((tm, MEM_W), F32)],
        compiler_params=pltpu.CompilerParams(dimension_semantics=("arbitrary",), vmem_limit_bytes=VMEM_LIMIT),
        name="memattn",
    )(x, g, wq, wo, bd, qw, mk, mv)


def _sink_softmax(s, sink):
    m = jnp.maximum(jnp.max(s, axis=-1, keepdims=True), sink)
    p = jnp.exp(s - m)
    return p / (jnp.sum(p, axis=-1, keepdims=True) + jnp.exp(sink - m))


def _swa_prompt_kernel(sink_ref, q_ref, kc_ref, kp_ref, vc_ref, vp_ref, bias_ref, o_ref):
    i = pl.program_id(1)
    tq = q_ref.shape[0]
    col = lax.broadcasted_iota(jnp.int32, (tq, 2 * tq), 1)
    first_valid = jnp.where(i > 0, 0, tq)
    colmask = col >= first_valid
    q = q_ref[...]
    outs = []
    for kv in range(ATT_KV):
        sl = slice(kv * HD_A, (kv + 1) * HD_A)
        kb = jnp.concatenate([kp_ref[:, sl], kc_ref[:, sl]], axis=0).astype(BF16)
        vb = jnp.concatenate([vp_ref[:, sl], vc_ref[:, sl]], axis=0).astype(BF16)
        for gg in range(ATT_GROUP):
            hh = kv * ATT_GROUP + gg
            s = _dot_nt(q[:, hh * HD_A:(hh + 1) * HD_A], kb) + bias_ref[hh]
            s = jnp.where(colmask, s, NEG)
            outs.append(_dot(_sink_softmax(s, sink_ref[hh]).astype(BF16), vb))
    o_ref[...] = jnp.concatenate(outs, axis=1).astype(BF16)


def _swa_prompt(sinks, q, k, v, bias, bsz, n):
    tq = WINDOW
    nq = n // tq
    cur = lambda width: pl.BlockSpec((tq, width), lambda b, i: (b * nq + i, 0))
    prev = lambda width: pl.BlockSpec((tq, width), lambda b, i: (b * nq + jnp.maximum(i - 1, 0), 0))
    return pl.pallas_call(
        _swa_prompt_kernel,
        grid=(bsz, nq),
        in_specs=[pl.BlockSpec(memory_space=pltpu.SMEM), cur(E_Q), cur(E_KV), prev(E_KV), cur(E_KV), prev(E_KV),
                  pl.BlockSpec(bias.shape, lambda b, i: (0, 0, 0))],
        out_specs=cur(E_Q),
        out_shape=jax.ShapeDtypeStruct((bsz * n, E_Q), BF16),
        compiler_params=pltpu.CompilerParams(dimension_semantics=("arbitrary", "arbitrary"), vmem_limit_bytes=VMEM_LIMIT),
        name="swa_prompt",
    )(sinks, q, k, k, v, v, bias)


def _swa_sample_kernel(n_seq, q_ref, wk_ref, wv_ref, kn_ref, vn_ref, bias_ref, sink_ref, o_ref):
    rows = q_ref.shape[1]
    pad = jnp.zeros((WINDOW - kn_ref.shape[1], E_KV), F32)
    first = lax.broadcasted_iota(jnp.int32, (rows, HD_A), 0) < rows // ATT_KV
    bias = bias_ref[...]
    sink = sink_ref[...]

    def one_seq(s, carry):
        kall = jnp.concatenate([wk_ref[s], kn_ref[s], pad], axis=0).astype(BF16)
        vall = jnp.concatenate([wv_ref[s], vn_ref[s], pad], axis=0).astype(BF16)
        w = _sink_softmax(_dot_nt(q_ref[s], kall) + bias, sink).astype(BF16)
        pv = _dot(w, vall)
        o_ref[s] = jnp.where(first, pv[:, :HD_A], pv[:, HD_A:])
        return carry

    lax.fori_loop(0, n_seq, one_seq, 0, unroll=SEQ_UNROLL)


def _swa_sample(q, wk, wv, kn, vn, bias, sink_col, n_seq):
    sb, rows, _ = q.shape
    lead = lambda a: pl.BlockSpec((n_seq,) + a.shape[1:], lambda i: (i, 0, 0))
    full = lambda a: pl.BlockSpec(a.shape, lambda i: (0,) * a.ndim)
    return pl.pallas_call(
        functools.partial(_swa_sample_kernel, n_seq),
        grid=(sb // n_seq,),
        in_specs=[lead(q), lead(wk), lead(wv), lead(kn), lead(vn), full(bias), full(sink_col)],
        out_specs=pl.BlockSpec((n_seq, rows, HD_A), lambda i: (i, 0, 0)),
        out_shape=jax.ShapeDtypeStruct((sb, rows, HD_A), F32),
        compiler_params=pltpu.CompilerParams(dimension_semantics=("arbitrary",), vmem_limit_bytes=VMEM_LIMIT),
        name="swa_sample",
    )(q, wk, wv, kn, vn, bias, sink_col)


def _ssd_chunk(xbc, prev, z, dtr, dt_t, hst, cw, cb, dtb, dtb_t, alog, alog_t, dsk, gnw):
    c = xbc.shape[0]
    row = lax.broadcasted_iota(jnp.int32, (c, CONV_DIM), 0)
    conv = cb + xbc * cw[CONV_W - 1:CONV_W]
    for s in range(1, CONV_W):
        shifted = jnp.where(row >= s, pltpu.roll(xbc, s, 0), pltpu.roll(prev, s, 0))
        conv = conv + shifted * cw[CONV_W - 1 - s:CONV_W - s]
    xc = _silu(conv)
    gw = SSM_GROUPS * SSM_N
    xs, bm, cm = xc[:, :SSM_INNER], xc[:, SSM_INNER:SSM_INNER + gw], xc[:, SSM_INNER + gw:]

    lane = lax.broadcasted_iota(jnp.int32, (1, LANES), 1)
    a_row = jnp.where(lane < SSM_HEADS, -jnp.exp(alog), 0.0)
    dt = _softplus(dtr + dtb)
    dta = dt * a_row
    dta_t = _softplus(dt_t + dtb_t) * (-jnp.exp(alog_t))
    ti = lax.broadcasted_iota(jnp.int32, (c, c), 0)
    si = lax.broadcasted_iota(jnp.int32, (c, c), 1)
    tri = ti >= si
    lower = jnp.where(tri, 1.0, 0.0).astype(BF16)
    upper = jnp.where(ti <= si, 1.0, 0.0).astype(BF16)
    cs_col = sum(_dot(lower, part) for part in _split3(dta))
    cs_row = sum(_dot(part, upper) for part in _split3(dta_t))
    ecs = jnp.exp(cs_col)
    cs_last = cs_col[c - 1:c, :]
    tail = jnp.exp(cs_last - cs_col)
    elast = jnp.exp(cs_last)

    ys, hs = [], []
    hpg = SSM_HEADS // SSM_GROUPS
    for g in range(SSM_GROUPS):
        bg = bm[:, g * SSM_N:(g + 1) * SSM_N]
        cgb = cm[:, g * SSM_N:(g + 1) * SSM_N].astype(BF16)
        gmat = _dot_nt(cgb, bg.astype(BF16))
        for hh in range(g * hpg, (g + 1) * hpg):
            seg = cs_col[:, hh:hh + 1] - cs_row[hh:hh + 1, :]
            decay = jnp.exp(jnp.where(tri, seg, NEG))
            xdt = (xs[:, hh * SSM_P:(hh + 1) * SSM_P] * dt[:, hh:hh + 1]).astype(BF16)
            h_h = hst[hh * SSM_P:(hh + 1) * SSM_P, :]
            ys.append(_dot((gmat * decay).astype(BF16), xdt) + _dot_nt(cgb, h_h.astype(BF16)) * ecs[:, hh:hh + 1])
            hs.append(h_h * elast[:, hh:hh + 1] + _dot_tn(xdt, (bg * tail[:, hh:hh + 1]).astype(BF16)))
    y = jnp.concatenate(ys, axis=1) + xs * dsk
    y = y * _silu(z)
    gn = SSM_INNER // SSM_GROUPS
    y = jnp.concatenate([_rms(y[:, g * gn:(g + 1) * gn], gnw[:, g * gn:(g + 1) * gn]) for g in range(SSM_GROUPS)], axis=1)
    return y, jnp.concatenate(hs, axis=0)


def _ssd_prompt_kernel(xbc_ref, prev_ref, z_ref, dt_ref, dtt_ref, cw_ref, cb_ref, dtb_ref, dtbt_ref, alog_ref, alogt_ref,
                       dsk_ref, gnw_ref, y_ref, hl_ref, hs_ref):
    c = pl.program_id(1)

    @pl.when(c == 0)
    def _():
        hs_ref[...] = jnp.zeros(hs_ref.shape, F32)

    prev = prev_ref[...] * (c > 0).astype(F32)
    y, hn = _ssd_chunk(xbc_ref[...], prev, z_ref[...], dt_ref[...], dtt_ref[0], hs_ref[...], cw_ref[...], cb_ref[...],
                       dtb_ref[...], dtbt_ref[...], alog_ref[...], alogt_ref[...], dsk_ref[...], gnw_ref[...])
    y_ref[...] = y.astype(BF16)
    hs_ref[...] = hn
    hl_ref[0] = hn


def _ssd_prompt(xbc, z, dt, dt_t, params, bsz, n):
    c = SSD_CHUNK
    nc = n // c
    cur = lambda width: pl.BlockSpec((c, width), lambda b, i: (b * nc + i, 0))
    full = lambda a: pl.BlockSpec(a.shape, lambda b, i: (0,) * a.ndim)
    hd = SSM_HEADS * SSM_P
    return pl.pallas_call(
        _ssd_prompt_kernel,
        grid=(bsz, nc),
        in_specs=[cur(CONV_DIM), pl.BlockSpec((c, CONV_DIM), lambda b, i: (b * nc + jnp.maximum(i - 1, 0), 0)),
                  cur(SSM_INNER), cur(LANES), pl.BlockSpec((1, SSM_HEADS, c), lambda b, i: (b, 0, i))]
        + [full(p) for p in params],
        out_specs=[cur(SSM_INNER), pl.BlockSpec((1, hd, SSM_N), lambda b, i: (b, 0, 0))],
        out_shape=[jax.ShapeDtypeStruct((bsz * n, SSM_INNER), BF16), jax.ShapeDtypeStruct((bsz, hd, SSM_N), F32)],
        scratch_shapes=[pltpu.VMEM((hd, SSM_N), F32)],
        compiler_params=pltpu.CompilerParams(dimension_semantics=("arbitrary", "arbitrary"), vmem_limit_bytes=VMEM_LIMIT),
        name="ssd_prompt",
    )(xbc, xbc, z, dt, dt_t, *params)


def _ssd_sample_kernel(n_seq, xbc_ref, st_ref, z_ref, dt_ref, dtt_ref, h0_ref, cw_ref, cb_ref, dtb_ref, dtbt_ref,
                       alog_ref, alogt_ref, dsk_ref, gnw_ref, y_ref, hn_ref):
    def one_seq(s, carry):
        y, hn = _ssd_chunk(xbc_ref[s], st_ref[s], z_ref[s], dt_ref[s], dtt_ref[s], h0_ref[s], cw_ref[...], cb_ref[...],
                           dtb_ref[...], dtbt_ref[...], alog_ref[...], alogt_ref[...], dsk_ref[...], gnw_ref[...])
        y_ref[s] = y
        hn_ref[s] = hn
        return carry

    lax.fori_loop(0, n_seq, one_seq, 0)


def _ssd_sample(xbc, st, z, dt, dt_t, h0, params, n_seq):
    sb = xbc.shape[0]
    lead = lambda a: pl.BlockSpec((n_seq,) + a.shape[1:], lambda i: (i, 0, 0))
    full = lambda a: pl.BlockSpec(a.shape, lambda i: (0,) * a.ndim)
    seq = (xbc, st, z, dt, dt_t, h0)
    return pl.pallas_call(
        functools.partial(_ssd_sample_kernel, n_seq),
        grid=(sb // n_seq,),
        in_specs=[lead(a) for a in seq] + [full(p) for p in params],
        out_specs=[lead(z), lead(h0)],
        out_shape=[jax.ShapeDtypeStruct(z.shape, F32), jax.ShapeDtypeStruct(h0.shape, F32)],
        compiler_params=pltpu.CompilerParams(dimension_semantics=("arbitrary",), vmem_limit_bytes=VMEM_LIMIT),
        name="ssd_sample",
    )(*seq, *params)


def _moba_prompt_kernel(q_ref, k_ref, v_ref, km_ref, down_ref, dprev_ref, cfar_ref, o_ref,
                        qa_ref, s_ref, p_ref, m_ref, al_ref, acc_ref):
    cur = pl.program_id(2)
    tq = q_ref.shape[0]
    rows = ATT_GROUP * tq
    qs = jnp.concatenate([q_ref[:, g * HD_C:(g + 1) * HD_C] for g in range(ATT_GROUP)], axis=0)
    lane = lax.broadcasted_iota(jnp.int32, (rows, LANES), 1)
    gate = jnp.where(lane < cur, _dot_nt(qs, km_ref[0, 0]), NEG)
    sel = _top3_mask(gate, lane.astype(F32))
    allowed = jnp.where(lane < cur, sel, jnp.where(lane == cur, 1.0, 0.0))
    c = cfar_ref[0]
    c_hi = c.astype(BF16)
    c_lo = (c - c_hi.astype(F32)).astype(BF16)
    upper = jnp.where(lane == CONST_LANE, c_hi, jnp.where(lane == CONST_LANE + 1, c_lo,
                                                         jnp.where(allowed > 0.5, 0.0, NEG).astype(BF16)))
    qa_ref[:, :HD_C] = qs
    qa_ref[:, HD_C:] = upper
    m_ref[...] = jnp.full(m_ref.shape, LOWEST, F32)
    acc_ref[...] = jnp.zeros(acc_ref.shape, F32)

    groups = [slice(i * rows // MXU_ROW_GROUPS, (i + 1) * rows // MXU_ROW_GROUPS) for i in range(MXU_ROW_GROUPS)]

    def score(j, slot):
        r0 = pl.multiple_of(j * MOBA_BLOCK, MOBA_BLOCK)
        kblk = k_ref[pl.ds(r0, MOBA_BLOCK), :]
        for gs in groups:
            s_ref[slot, gs, :] = _dot_nt(qa_ref[gs, :], kblk)

    def consume(j, slot, delta_ref):
        for r in range(rows // SUB_ROWS):
            sl = slice(r * SUB_ROWS, (r + 1) * SUB_ROWS)
            halves = [s_ref[slot, sl, hf * LANES:(hf + 1) * LANES] for hf in range(2)]
            if delta_ref is not None:
                halves = [sh + delta_ref[0, sl, hf * LANES:(hf + 1) * LANES] for hf, sh in enumerate(halves)]
            m_old = m_ref[sl, :]
            m_new = jnp.maximum(m_old, jnp.max(jnp.maximum(halves[0], halves[1]), axis=-1, keepdims=True))
            for hf, sh in enumerate(halves):
                p_ref[sl, hf * LANES:(hf + 1) * LANES] = jnp.exp2(sh - m_new).astype(BF16)
            al_ref[sl, :] = jnp.exp2(m_old - m_new)
            m_ref[sl, :] = m_new
        r0 = pl.multiple_of(j * MOBA_BLOCK, MOBA_BLOCK)
        vblk = v_ref[pl.ds(r0, MOBA_BLOCK), :]
        for gs in groups:
            acc_ref[gs, :] = jnp.concatenate([al_ref[gs, :]] * 2, axis=1) * acc_ref[gs, :] + _dot(p_ref[gs, :], vblk)

    n_far = jnp.maximum(cur - 1, 0)
    last = jnp.maximum(n_far - 1, 0)
    score(cur, 0)

    @pl.when(cur == 0)
    def _():
        consume(cur, 0, down_ref)

    @pl.when(cur >= 1)
    def _():
        score(cur - 1, 1)
        consume(cur, 0, down_ref)
        score(0, 0)
        consume(cur - 1, 1, dprev_ref)

    def far_pair(t, carry):
        j = 2 * t
        score(jnp.minimum(j + 1, last), 1)
        consume(j, 0, None)

        @pl.when(j + 1 < n_far)
        def _():
            score(jnp.minimum(j + 2, last), 0)
            consume(j + 1, 1, None)

        return carry

    lax.fori_loop(0, (n_far + 1) // 2, far_pair, 0)
    o = acc_ref[:, :HD_C] / acc_ref[:, HD_C:]
    for g in range(ATT_GROUP):
        o_ref[:, g * HD_C:(g + 1) * HD_C] = o[g * tq:(g + 1) * tq].astype(BF16)


def _moba_prompt(q, kaug, vaug, kmean, down, dprev, cfar, bsz, n):
    tq = MOBA_BLOCK
    nq = n // tq
    assert nq <= CONST_LANE
    rows = ATT_GROUP * tq
    gw = ATT_GROUP * HD_C
    qspec = pl.BlockSpec((tq, gw), lambda b, kv, i: (b * nq + i, kv))
    kvspec = pl.BlockSpec((n, 2 * HD_C), lambda b, kv, i: (b, kv))
    tab = lambda a: pl.BlockSpec((1,) + a.shape[1:], lambda b, kv, i: (kv, 0, 0))
    return pl.pallas_call(
        _moba_prompt_kernel,
        grid=(bsz, ATT_KV, nq),
        in_specs=[qspec, kvspec, kvspec, pl.BlockSpec((1, 1, LANES, HD_C), lambda b, kv, i: (b, kv, 0, 0)),
                  tab(down), tab(dprev), tab(cfar)],
        out_specs=qspec,
        out_shape=jax.ShapeDtypeStruct((bsz * n, C_Q), BF16),
        scratch_shapes=[pltpu.VMEM((rows, 2 * HD_C), BF16), pltpu.VMEM((2, rows, MOBA_BLOCK), F32),
                        pltpu.VMEM((rows, MOBA_BLOCK), BF16), pltpu.VMEM((rows, LANES), F32),
                        pltpu.VMEM((rows, LANES), F32), pltpu.VMEM((rows, 2 * HD_C), F32)],
        compiler_params=pltpu.CompilerParams(dimension_semantics=("arbitrary",) * 3, vmem_limit_bytes=VMEM_LIMIT),
        name="moba_prompt",
    )(q, kaug, vaug, kmean, down, dprev, cfar)


def _moba_sample_kernel(nbs, ppb, pt_ref, q_ref, *refs):
    npg = nbs * ppb
    k_pages, v_pages = refs[:npg], refs[npg:2 * npg]
    kn_ref, vn_ref, tbl_ref, bown_ref, o_ref, gate_sc, m_sc, l_sc, o_sc = refs[2 * npg:]
    step = pl.program_id(1)
    nblk = pl.num_programs(1) * nbs
    rows = q_ref.shape[1]
    lane = lax.broadcasted_iota(jnp.int32, (rows, LANES), 1)

    @pl.when(step == 0)
    def _():
        gate_sc[...] = jnp.zeros(gate_sc.shape, F32)
        m_sc[...] = jnp.zeros(m_sc.shape, F32)
        l_sc[...] = jnp.zeros(l_sc.shape, F32)

    q = q_ref[0]
    half = rows // ATT_KV
    lane_h = lax.broadcasted_iota(jnp.int32, (half, LANES), 1)
    own = lambda pages, kv: jnp.concatenate([r[pl.ds(kv, PAGE_SIZE, stride=ATT_KV), :] for r in pages], axis=0)
    gp = SCORE_BLOCKS * ppb
    units = [(kv, g) for g in range(nbs // SCORE_BLOCKS) for kv in range(ATT_KV)]
    keys = [own(k_pages[g * gp:(g + 1) * gp], kv) for kv, g in units]
    scores = [_dot_nt(q[kv * half:(kv + 1) * half], kk.astype(BF16)) for (kv, g), kk in zip(units, keys)]
    for (kv, g), kk, s_all in zip(units, keys, scores):
        rs = slice(kv * half, (kv + 1) * half)
        vv = own(v_pages[g * gp:(g + 1) * gp], kv)
        qf = q[rs].astype(F32)
        gate_n, m_n, l_n = gate_sc[rs, :], m_sc[rs, :], l_sc[rs, :]
        for i in range(SCORE_BLOCKS):
            j = step * nbs + g * SCORE_BLOCKS + i
            cs = slice(i * MOBA_BLOCK, (i + 1) * MOBA_BLOCK)
            kmean = (jnp.sum(kk[cs], axis=0, keepdims=True) * (1.0 / MOBA_BLOCK)).astype(BF16).astype(F32)
            gate = jnp.sum(qf * kmean, axis=-1, keepdims=True)
            s = s_all[:, cs] + tbl_ref[jnp.where(j == nblk - 1, 1, 0), rs, :]
            m = jnp.max(s, axis=-1, keepdims=True)
            p = jnp.exp2(s - m)
            hit = lane_h == j
            gate_n = jnp.where(hit, gate, gate_n)
            m_n = jnp.where(hit, m, m_n)
            l_n = jnp.where(hit, jnp.sum(p, axis=-1, keepdims=True), l_n)
            o_sc[j, rs, :] = _dot(p.astype(BF16), vv[cs].astype(BF16))
        gate_sc[rs, :], m_sc[rs, :], l_sc[rs, :] = gate_n, m_n, l_n

    @pl.when(step == pl.num_programs(1) - 1)
    def _():
        pad = jnp.zeros((LANES - kn_ref.shape[1], HD_C), F32)
        sel = _top3_mask(jnp.where(lane < nblk, gate_sc[...], LOWEST), lane.astype(F32)) > 0.5
        m_blk = jnp.where(sel, m_sc[...], LOWEST)
        kn = jnp.concatenate([kn_ref[0], pad], axis=0).astype(BF16)
        vn = jnp.concatenate([vn_ref[0], pad], axis=0).astype(BF16)
        s_own = _dot_nt(q, kn) + bown_ref[...]
        m_tot = jnp.maximum(jnp.max(m_blk, axis=-1, keepdims=True), jnp.max(s_own, axis=-1, keepdims=True))
        w = jnp.exp2(m_blk - m_tot)
        p_own = jnp.exp2(s_own - m_tot)
        den = jnp.sum(w * l_sc[...], axis=-1, keepdims=True) + jnp.sum(p_own, axis=-1, keepdims=True)
        o = _dot(p_own.astype(BF16), vn)
        for b in range(o_sc.shape[0]):
            o = o + w[:, b:b + 1] * o_sc[b]
        o_ref[0] = o / den


def _moba_sample(pt, q, pool_k, pool_v, kn, vn, tbl, bown, n_pages):
    sb, rows, _ = q.shape
    ppb = MOBA_BLOCK // PAGE_SIZE
    nblk = n_pages // ppb
    nbs = SAMPLE_BLOCKS_PER_STEP
    assert nblk % nbs == 0 and nblk <= LANES

    def page(p):
        return pl.BlockSpec((PAGE_SIZE * ATT_KV, HD_C), lambda b, s, pt_ref: (pt_ref[b * n_pages + s * nbs * ppb + p], 0))

    seq = lambda a: pl.BlockSpec((1,) + a.shape[1:], lambda b, s, pt_ref: (b,) + (0,) * (a.ndim - 1))
    full = lambda a: pl.BlockSpec(a.shape, lambda b, s, pt_ref: (0,) * a.ndim)
    pages = [page(p) for p in range(nbs * ppb)]
    grid_spec = pltpu.PrefetchScalarGridSpec(
        num_scalar_prefetch=1,
        grid=(sb, nblk // nbs),
        in_specs=[seq(q)] + pages * 2 + [seq(kn), seq(vn), full(tbl), full(bown)],
        out_specs=seq(q),
        scratch_shapes=[pltpu.VMEM((rows, LANES), F32)] * 3 + [pltpu.VMEM((nblk, rows, HD_C), F32)],
    )
    return pl.pallas_call(
        functools.partial(_moba_sample_kernel, nbs, ppb),
        grid_spec=grid_spec,
        out_shape=jax.ShapeDtypeStruct(q.shape, F32),
        compiler_params=pltpu.CompilerParams(dimension_semantics=("arbitrary", "arbitrary"), vmem_limit_bytes=VMEM_LIMIT),
        name="moba_sample",
    )(pt, q, *([pool_k] * len(pages)), *([pool_v] * len(pages)), kn, vn, tbl, bown)


def _bias_table(tab_t, head, dist, valid):
    rows_tab = jnp.take(tab_t, jnp.asarray(np.asarray(head, np.int32)), axis=0)
    bucket = jnp.asarray(_bucket_np(np.broadcast_to(dist, np.broadcast_shapes(np.shape(dist), np.shape(valid)))))
    out = jnp.zeros(bucket.shape, F32)
    for b in range(N_BUCKETS):
        out = jnp.where(bucket == b, rows_tab[:, b:b + 1], out)
    return jnp.where(jnp.asarray(np.broadcast_to(valid, bucket.shape)), out, NEG)


def _head_rows(per_head):
    return np.repeat(np.arange(ATT_HEADS), per_head)


def _to_group_rows(a, sb, sn, hd):
    return a.reshape(sb, sn, ATT_KV, ATT_GROUP, hd).transpose(0, 2, 3, 1, 4).reshape(sb, ATT_KV, ATT_GROUP * sn, hd)


def _from_group_rows(a, sb, sn, hd):
    return a.reshape(sb, ATT_KV, ATT_GROUP, sn, hd).transpose(0, 3, 1, 2, 4).reshape(sb * sn, ATT_HEADS * hd)


def _pad_lanes(v):
    return jnp.pad(v.astype(F32), (0, LANES - v.shape[0]))[None]


def kernel(x_prompt, x_sample, cache_win_k, cache_win_v, state_conv, state_ssm, cache_moba_k, cache_moba_v, cache_mem_k, cache_mem_v, page_table, mem_prompt, rel_bias, norm_mix, norm_mem, norm_ffn, even_w_in, even_w_out, even_q_norm, even_k_norm, even_sinks, even_conv_w, even_conv_b, even_dt_bias, even_a_log, even_d_skip, even_gate_norm, odd_w_in, odd_w_out, odd_q_norm, odd_k_norm, mem_wq, mem_wk, mem_wv, mem_wo, mem_q_norm, mem_k_norm, ffn_w_gate, ffn_w_up, ffn_w_down):
    bsz, n = x_prompt.shape[:2]
    sb, sn = x_sample.shape[:2]
    depth = norm_mix.shape[0]
    rp, rs = bsz * n, sb * sn
    n_pages = page_table.shape[1]
    n_pool = cache_moba_k.shape[1]
    past_len = n_pages * PAGE_SIZE
    assert past_len % MOBA_BLOCK == 0 and cache_win_k.shape[2] == WINDOW and sn >= CONV_W - 1
    seq_group = 16

    x = jnp.concatenate([x_prompt.reshape(rp, D_MODEL), x_sample.reshape(rs, D_MODEL)], axis=0)
    tab_t = rel_bias.T.astype(F32)

    qi = np.tile(np.arange(WINDOW), ATT_HEADS)[:, None]
    kj = np.arange(2 * WINDOW)[None, :]
    d = WINDOW + qi - kj
    swa_bias_p = _bias_table(tab_t, _head_rows(WINDOW), d, (d >= 0) & (d < WINDOW)).reshape(ATT_HEADS, WINDOW, 2 * WINDOW)
    ti = np.tile(np.arange(sn), ATT_HEADS)[:, None]
    d = np.where(kj < WINDOW, WINDOW + ti - kj, ti - (kj - WINDOW))
    swa_bias_s = _bias_table(tab_t, _head_rows(sn), d, (d >= 0) & (d < WINDOW) & (kj < WINDOW + sn))
    assert _bucket_np(MOBA_BLOCK + 1) == N_BUCKETS - 1
    far_bias = tab_t[:, N_BUCKETS - 1] * LOG2E
    qi = np.tile(np.arange(MOBA_BLOCK), ATT_HEADS)[:, None]
    kj = np.arange(MOBA_BLOCK)[None, :]
    heads_p = _head_rows(MOBA_BLOCK)
    cfar_p = jnp.take(far_bias, jnp.asarray(heads_p))[:, None]
    grp = lambda t: t.reshape(ATT_KV, ATT_GROUP * MOBA_BLOCK, t.shape[-1])
    down_p = grp(jnp.where(jnp.asarray(qi >= kj), _bias_table(tab_t, heads_p, qi - kj, True) * LOG2E - cfar_p, NEG))
    dprev_p = grp(_bias_table(tab_t, heads_p, MOBA_BLOCK + qi - kj, True) * LOG2E - cfar_p)
    cfar_p = grp(jnp.broadcast_to(cfar_p, (cfar_p.shape[0], LANES)))
    heads_s = _head_rows(sn)
    row_kv = (heads_s // ATT_GROUP)[:, None]
    cfar_s = jnp.take(far_bias, jnp.asarray(heads_s))[:, None]
    tbl_s = jnp.stack([jnp.broadcast_to(cfar_s, (cfar_s.shape[0], MOBA_BLOCK)),
                       _bias_table(tab_t, heads_s, MOBA_BLOCK + ti - kj, True) * LOG2E])
    col = np.arange(LANES)[None, :]
    tok = col // ATT_KV
    bown_s = _bias_table(tab_t, heads_s, ti - tok, (row_kv == col % ATT_KV) & (ti >= tok) & (tok < sn)) * LOG2E

    pool_k = cache_moba_k.reshape(-1, HD_C)
    pool_v = cache_moba_v.reshape(-1, HD_C)
    pt_flat = page_table.reshape(-1)
    mem_k = cache_mem_k.transpose(0, 1, 3, 4, 2).reshape(depth * sb, MEM_W, MEM_LEN)
    mem_v = cache_mem_v.transpose(0, 1, 3, 4, 2).reshape(depth * sb, MEM_W, MEM_LEN)
    mem_rows = mem_prompt.reshape(bsz * MEM_LEN, D_MODEL)

    pwk, pwv, pcv, pss, pmk, pmv, pxk, pxv = [], [], [], [], [], [], [], []
    swk, swv, scv, sss, smk, smv = [], [], [], [], [], []
    for l in range(depth):
        g_mix = norm_mix[l][None]
        if l % 2 == 0:
            e = l // 2
            w_in = jnp.pad(even_w_in[e], ((0, 0), (0, EVEN_W - even_w_in.shape[2]))).astype(BF16)
            w_out = even_w_out[e].astype(BF16)
            qw = jnp.tile(even_q_norm[e], ATT_HEADS)[None]
            kw = jnp.tile(even_k_norm[e], ATT_KV)[None]
            q, k, v, z, xbc, dt = _proj_even(x, g_mix, w_in, qw, kw)
            params = (even_conv_w[e], even_conv_b[e][None], _pad_lanes(even_dt_bias[e]), even_dt_bias[e][:, None],
                      _pad_lanes(even_a_log[e]), even_a_log[e][:, None], jnp.repeat(even_d_skip[e], SSM_P)[None],
                      even_gate_norm[e][None])
            att_p = _swa_prompt(even_sinks[e], q, k, v, swa_bias_p, bsz, n)
            dt_t = dt[:rp, :SSM_HEADS].reshape(bsz, n, SSM_HEADS).transpose(0, 2, 1)
            y_p, h_p = _ssd_prompt(xbc, z, dt, dt_t, params, bsz, n)
            sink_col = jnp.repeat(even_sinks[e], sn)[:, None]
            k_s = k[rp:].reshape(sb, sn, E_KV)
            v_s = v[rp:].reshape(sb, sn, E_KV)
            qg = _to_group_rows(q[rp:], sb, sn, HD_A)
            zq = jnp.zeros_like(qg[:, 0])
            q_bd = jnp.concatenate([jnp.concatenate([qg[:, 0], zq], axis=-1),
                                    jnp.concatenate([zq, qg[:, 1]], axis=-1)], axis=1)
            att_s = _swa_sample(q_bd, cache_win_k[e].reshape(sb, WINDOW, E_KV), cache_win_v[e].reshape(sb, WINDOW, E_KV),
                                k_s, v_s, swa_bias_s, sink_col, seq_group)
            xbc_s = xbc[rp:].reshape(sb, sn, CONV_DIM)
            st = jnp.pad(state_conv[e], ((0, 0), (sn - (CONV_W - 1), 0), (0, 0)))
            dt_s = dt[rp:].reshape(sb, sn, LANES)
            y_s, h_s = _ssd_sample(xbc_s, st, z[rp:].reshape(sb, sn, SSM_INNER), dt_s,
                                   dt_s[:, :, :SSM_HEADS].transpose(0, 2, 1),
                                   state_ssm[e].reshape(sb, SSM_HEADS * SSM_P, SSM_N), params, seq_group)
            att = jnp.concatenate([att_p, _from_group_rows(att_s, sb, sn, HD_A).astype(BF16)], axis=0)
            yy = jnp.concatenate([y_p, y_s.reshape(rs, SSM_INNER).astype(BF16)], axis=0)
            x = _outproj(x, [att, yy], [w_out[:E_Q], w_out[E_Q:]])
            tail = lambda a, cnt: jnp.stack([a[(b + 1) * n - cnt:(b + 1) * n] for b in range(bsz)])
            pwk.append(tail(k, WINDOW).reshape(bsz, WINDOW, ATT_KV, HD_A))
            pwv.append(tail(v, WINDOW).reshape(bsz, WINDOW, ATT_KV, HD_A))
            pcv.append(tail(xbc, CONV_W - 1))
            pss.append(h_p.reshape(bsz, SSM_HEADS, SSM_P, SSM_N))
            swk.append(jnp.concatenate([cache_win_k[e], k_s.reshape(sb, sn, ATT_KV, HD_A)], axis=1)[:, -WINDOW:])
            swv.append(jnp.concatenate([cache_win_v[e], v_s.reshape(sb, sn, ATT_KV, HD_A)], axis=1)[:, -WINDOW:])
            scv.append(jnp.concatenate([state_conv[e], xbc_s], axis=1)[:, -(CONV_W - 1):])
            sss.append(h_s.reshape(sb, SSM_HEADS, SSM_P, SSM_N))
        else:
            o = l // 2
            w_in = odd_w_in[o].astype(BF16)
            w_out = odd_w_out[o].astype(BF16)
            nb = n // MOBA_BLOCK
            q, k, v, kaug, vaug, km = _proj_odd(x, g_mix, w_in, odd_q_norm[o][None], odd_k_norm[o][None], nb)
            kmean = km[:bsz * nb, 0].reshape(bsz, nb, ATT_KV, HD_C).transpose(0, 2, 1, 3)
            kmean = jnp.pad(kmean, ((0, 0), (0, 0), (0, LANES - nb), (0, 0))).astype(BF16)
            att_p = _moba_prompt(q, kaug, vaug, kmean, down_p, dprev_p, cfar_p, bsz, n)
            k_s = k[rp:].reshape(sb, sn, C_KV)
            v_s = v[rp:].reshape(sb, sn, C_KV)
            q_s = _to_group_rows(q[rp:], sb, sn, HD_C).reshape(sb, ATT_HEADS * sn, HD_C)
            att_s = _moba_sample(pt_flat + o * n_pool, q_s, pool_k, pool_v, k_s.reshape(sb, sn * ATT_KV, HD_C),
                                 v_s.reshape(sb, sn * ATT_KV, HD_C), tbl_s, bown_s, n_pages)
            att = jnp.concatenate([att_p, _from_group_rows(att_s, sb, sn, HD_C).astype(BF16)], axis=0)
            x = _outproj(x, [att], [w_out])
            pmk.append(k[:rp].reshape(bsz, n, ATT_KV, HD_C))
            pmv.append(v[:rp].reshape(bsz, n, ATT_KV, HD_C))
            smk.append(k_s.reshape(sb, sn, ATT_KV, HD_C))
            smv.append(v_s.reshape(sb, sn, ATT_KV, HD_C))
        mqw = jnp.tile(mem_q_norm[l], MEM_HEADS)[None]
        mkw = jnp.tile(mem_k_norm[l], MEM_HEADS)[None]
        wq = mem_wq[l].astype(BF16)
        wo = mem_wo[l].astype(BF16)
        xk, xv = _memkv(mem_rows, mem_wk[l].astype(BF16), mem_wv[l].astype(BF16), mkw)
        pxk.append(xk.reshape(bsz, MEM_LEN, MEM_HEADS, MEM_HD))
        pxv.append(xv.reshape(bsz, MEM_LEN, MEM_HEADS, MEM_HD))
        g_mem = norm_mem[l][None]
        x = _memattn(x, g_mem, wq, wo, mqw, xk.reshape(bsz, MEM_LEN, MEM_W), xv.reshape(bsz, MEM_LEN, MEM_W),
                     0, bsz, 0, 1, ROW_TILE, n // ROW_TILE)
        x = _memattn(x, g_mem, wq, wo, mqw, mem_k, mem_v, rp, sb, l * sb, seq_group, sn, 1)
        x = _ffn(x, norm_ffn[l][None], ffn_w_gate[l].astype(BF16), ffn_w_up[l].astype(BF16), ffn_w_down[l].astype(BF16))

    return (x[:rp].reshape(bsz, n, D_MODEL), x[rp:].reshape(sb, sn, D_MODEL),
            jnp.stack(pwk), jnp.stack(pwv), jnp.stack(pcv), jnp.stack(pss), jnp.stack(pmk), jnp.stack(pmv),
            jnp.stack(pxk), jnp.stack(pxv), jnp.stack(swk), jnp.stack(swv), jnp.stack(scv), jnp.stack(sss),
            jnp.stack(smk), jnp.stack(smv))
```

```python
import functools
import math

import jax
import jax.numpy as jnp
import numpy as np
from jax import lax
from jax.experimental import pallas as pl
from jax.experimental.pallas import tpu as pltpu

F32 = jnp.float32
BF16 = jnp.bfloat16

D_MODEL = 1024
ATT_HEADS = 8
ATT_KV = 2
ATT_GROUP = ATT_HEADS // ATT_KV
N_BUCKETS = 32
MAX_DIST = 128
WINDOW = 128
HD_A = 64
SSM_HEADS = 8
SSM_P = 64
SSM_INNER = SSM_HEADS * SSM_P
SSM_GROUPS = 2
SSM_N = 64
CONV_W = 4
CONV_DIM = SSM_INNER + 2 * SSM_GROUPS * SSM_N
SSD_CHUNK = 128
HD_C = 128
MOBA_BLOCK = 256
MOBA_TOPK = 3
PAGE_SIZE = 128
MEM_LEN = 256
MEM_HEADS = 4
MEM_HD = 64
MEM_W = MEM_HEADS * MEM_HD
E_Q = ATT_HEADS * HD_A
E_KV = ATT_KV * HD_A
C_Q = ATT_HEADS * HD_C
C_KV = ATT_KV * HD_C
EPS = 1e-6
NEG = -1e30
LOWEST = -3e38

LOG2E = 1.4426950408889634
LANES = 128
CONST_LANE = LANES - 2
SUB_ROWS = 128
MXU_ROW_GROUPS = 2
SAMPLE_BLOCKS_PER_STEP = 16
SCORE_BLOCKS = 4
SEQ_UNROLL = 4
EVEN_W = E_Q + 2 * E_KV + SSM_INNER + CONV_DIM + LANES
ROW_TILE = 512
FFN_ROW_TILE = 1024
FFN_COL_TILE = 1408
VMEM_LIMIT = 56 * 1024 * 1024


def _dot(a, b):
    return jnp.dot(a, b, preferred_element_type=F32)


def _dot_nt(a, b):
    return lax.dot_general(a, b, (((1,), (1,)), ((), ())), preferred_element_type=F32)


def _dot_tn(a, b):
    return lax.dot_general(a, b, (((0,), (0,)), ((), ())), preferred_element_type=F32)


def _split3(x):
    hi = x.astype(BF16)
    r = x - hi.astype(F32)
    mid = r.astype(BF16)
    lo = (r - mid.astype(F32)).astype(BF16)
    return hi, mid, lo


def _rms(x, w):
    ms = jnp.mean(x * x, axis=-1, keepdims=True)
    return x * lax.rsqrt(ms + EPS) * w


def _headnorm_bd(y, ones_bd, w, hd):
    sq = y * y
    hi = sq.astype(BF16)
    lo = (sq - hi.astype(F32)).astype(BF16)
    ss = _dot(hi, ones_bd) + _dot(lo, ones_bd)
    return y * lax.rsqrt(ss * (1.0 / hd) + EPS) * w


def _silu(x):
    return x * jax.nn.sigmoid(x)


def _softplus(x):
    return jnp.maximum(x, 0.0) + jnp.log1p(jnp.exp(-jnp.abs(x)))


def _bucket_np(dist):
    n = np.maximum(np.asarray(dist, np.int64), 0)
    nf = np.maximum(n, 1).astype(np.float64)
    exact = N_BUCKETS // 2
    large = exact + np.floor(np.log(nf / exact) / math.log(MAX_DIST / exact) * (N_BUCKETS - exact) + 1e-6).astype(np.int64)
    return np.where(n < exact, n, np.minimum(large, N_BUCKETS - 1)).astype(np.int32)


def _block_diag_ones(width, hd):
    i = np.arange(width) // hd
    return jnp.asarray((i[:, None] == i[None, :]).astype(np.float32), BF16)


def _top3_mask(gate, lane_f):
    sel = jnp.zeros_like(gate)
    for _ in range(MOBA_TOPK):
        mx = jnp.max(gate, axis=-1, keepdims=True)
        idx = jnp.min(jnp.where(gate == mx, lane_f, 1e9), axis=-1, keepdims=True)
        hit = lane_f == idx
        sel = jnp.where(hit, 1.0, sel)
        gate = jnp.where(hit, LOWEST, gate)
    return sel


def _proj_even_kernel(x_ref, g_ref, w_ref, bdq_ref, bdk_ref, qw_ref, kw_ref,
                      q_ref, k_ref, v_ref, z_ref, xbc_ref, dt_ref):
    h = _rms(x_ref[...], g_ref[...]).astype(BF16)
    y = _dot(h, w_ref[...])
    q = _headnorm_bd(y[:, :E_Q], bdq_ref[...], qw_ref[...], HD_A)
    q_ref[...] = (q * HD_A ** -0.5).astype(BF16)
    o = E_Q
    k_ref[...] = _headnorm_bd(y[:, o:o + E_KV], bdk_ref[...], kw_ref[...], HD_A)
    o += E_KV
    v_ref[...] = y[:, o:o + E_KV]
    o += E_KV
    z_ref[...] = y[:, o:o + SSM_INNER]
    o += SSM_INNER
    xbc_ref[...] = y[:, o:o + CONV_DIM]
    o += CONV_DIM
    dt_ref[...] = y[:, o:o + LANES]


def _proj_even(x, g, w, qw, kw):
    rows = x.shape[0]
    tm = ROW_TILE
    row = lambda width: pl.BlockSpec((tm, width), lambda i: (i, 0))
    full = lambda a: pl.BlockSpec(a.shape, lambda i: (0,) * a.ndim)
    bdq = _block_diag_ones(E_Q, HD_A)
    bdk = _block_diag_ones(E_KV, HD_A)
    ins = (x, g, w, bdq, bdk, qw, kw)
    widths = (E_Q, E_KV, E_KV, SSM_INNER, CONV_DIM, LANES)
    dts = (BF16, F32, F32, F32, F32, F32)
    return pl.pallas_call(
        _proj_even_kernel,
        grid=(rows // tm,),
        in_specs=[row(D_MODEL)] + [full(a) for a in ins[1:]],
        out_specs=[row(wd) for wd in widths],
        out_shape=[jax.ShapeDtypeStruct((rows, wd), dt) for wd, dt in zip(widths, dts)],
        compiler_params=pltpu.CompilerParams(dimension_semantics=("arbitrary",), vmem_limit_bytes=VMEM_LIMIT),
        name="proj_even",
    )(*ins)


def _proj_odd_kernel(seq_blocks, x_ref, g_ref, w_ref, qw_ref, kw_ref,
                     q_ref, k_ref, v_ref, kaug_ref, vaug_ref, km_ref):
    i = pl.program_id(0)
    tm = x_ref.shape[0]
    h = _rms(x_ref[...], g_ref[...]).astype(BF16)
    y = _dot(h, w_ref[...])
    qw = qw_ref[...]
    for hh in range(ATT_HEADS):
        sl = slice(hh * HD_C, (hh + 1) * HD_C)
        q_ref[:, sl] = (_rms(y[:, sl], qw) * (HD_C ** -0.5 * LOG2E)).astype(BF16)
    nblk = tm // MOBA_BLOCK
    row = lax.broadcasted_iota(jnp.int32, (tm, LANES), 0)
    lane = lax.broadcasted_iota(jnp.int32, (tm, LANES), 1)
    blk = (i * nblk) % seq_blocks + sum(jnp.where(row >= b * MOBA_BLOCK, 1, 0) for b in range(1, nblk))
    onehot = jnp.where(jnp.logical_or(lane == blk, lane >= CONST_LANE), 1.0, 0.0).astype(BF16)
    ones = jnp.ones((tm, LANES), BF16)
    kw = kw_ref[...]
    for kv in range(ATT_KV):
        sl = slice(kv * HD_C, (kv + 1) * HD_C)
        kh = _rms(y[:, C_Q + kv * HD_C:C_Q + (kv + 1) * HD_C], kw)
        vh = y[:, C_Q + C_KV + kv * HD_C:C_Q + C_KV + (kv + 1) * HD_C]
        k_ref[:, sl] = kh
        v_ref[:, sl] = vh
        a = 2 * kv * HD_C
        kaug_ref[:, a:a + HD_C] = kh.astype(BF16)
        kaug_ref[:, a + HD_C:a + 2 * HD_C] = onehot
        vaug_ref[:, a:a + HD_C] = vh.astype(BF16)
        vaug_ref[:, a + HD_C:a + 2 * HD_C] = ones
        for b in range(nblk):
            km_ref[b, :, sl] = jnp.mean(kh[b * MOBA_BLOCK:(b + 1) * MOBA_BLOCK], axis=0, keepdims=True)


def _proj_odd(x, g, w, qw, kw, seq_blocks):
    rows = x.shape[0]
    tm = ROW_TILE
    nblk = tm // MOBA_BLOCK
    row = lambda width: pl.BlockSpec((tm, width), lambda i: (i, 0))
    full = lambda a: pl.BlockSpec(a.shape, lambda i: (0,) * a.ndim)
    ins = (x, g, w, qw, kw)
    widths = (C_Q, C_KV, C_KV, 2 * C_KV, 2 * C_KV)
    dts = (BF16, F32, F32, BF16, BF16)
    return pl.pallas_call(
        functools.partial(_proj_odd_kernel, seq_blocks),
        grid=(rows // tm,),
        in_specs=[row(D_MODEL)] + [full(a) for a in ins[1:]],
        out_specs=[row(wd) for wd in widths] + [pl.BlockSpec((nblk, 1, C_KV), lambda i: (i, 0, 0))],
        out_shape=[jax.ShapeDtypeStruct((rows, wd), dt) for wd, dt in zip(widths, dts)]
        + [jax.ShapeDtypeStruct((rows // MOBA_BLOCK, 1, C_KV), F32)],
        compiler_params=pltpu.CompilerParams(dimension_semantics=("arbitrary",), vmem_limit_bytes=VMEM_LIMIT),
        name="proj_odd",
    )(*ins)


def _outproj_kernel(n_in, x_ref, *refs):
    o_ref = refs[-1]
    acc = x_ref[...]
    for a_ref, w_ref in zip(refs[:n_in], refs[n_in:2 * n_in]):
        acc = acc + _dot(a_ref[...], w_ref[...])
    o_ref[...] = acc


def _outproj(x, acts, ws):
    rows = x.shape[0]
    tm = ROW_TILE
    row = lambda width: pl.BlockSpec((tm, width), lambda i: (i, 0))
    full = lambda a: pl.BlockSpec(a.shape, lambda i: (0,) * a.ndim)
    return pl.pallas_call(
        functools.partial(_outproj_kernel, len(acts)),
        grid=(rows // tm,),
        in_specs=[row(D_MODEL)] + [row(a.shape[1]) for a in acts] + [full(w) for w in ws],
        out_specs=row(D_MODEL),
        out_shape=jax.ShapeDtypeStruct((rows, D_MODEL), F32),
        compiler_params=pltpu.CompilerParams(dimension_semantics=("arbitrary",), vmem_limit_bytes=VMEM_LIMIT),
        name="outproj",
    )(x, *acts, *ws)


def _ffn_kernel(x_ref, g_ref, wg_ref, wu_ref, wd_ref, o_ref, hn_ref, acc_ref):
    j = pl.program_id(1)

    @pl.when(j == 0)
    def _():
        x = x_ref[...]
        hn_ref[...] = _rms(x, g_ref[...]).astype(BF16)
        acc_ref[...] = x

    h = hn_ref[...]
    a = (_silu(_dot(h, wg_ref[...])) * _dot(h, wu_ref[...])).astype(BF16)
    acc_ref[...] += _dot(a, wd_ref[...])

    @pl.when(j == pl.num_programs(1) - 1)
    def _():
        o_ref[...] = acc_ref[...]


def _ffn(x, g, wg, wu, wd):
    rows = x.shape[0]
    d_ff = wg.shape[1]
    tm = FFN_ROW_TILE if rows % FFN_ROW_TILE == 0 else ROW_TILE
    tf = FFN_COL_TILE
    return pl.pallas_call(
        _ffn_kernel,
        grid=(rows // tm, d_ff // tf),
        in_specs=[
            pl.BlockSpec((tm, D_MODEL), lambda i, j: (i, 0)),
            pl.BlockSpec((1, D_MODEL), lambda i, j: (0, 0)),
            pl.BlockSpec((D_MODEL, tf), lambda i, j: (0, j)),
            pl.BlockSpec((D_MODEL, tf), lambda i, j: (0, j)),
            pl.BlockSpec((tf, D_MODEL), lambda i, j: (j, 0)),
        ],
        out_specs=pl.BlockSpec((tm, D_MODEL), lambda i, j: (i, 0)),
        out_shape=jax.ShapeDtypeStruct((rows, D_MODEL), F32),
        scratch_shapes=[pltpu.VMEM((tm, D_MODEL), BF16), pltpu.VMEM((tm, D_MODEL), F32)],
        compiler_params=pltpu.CompilerParams(dimension_semantics=("arbitrary", "arbitrary"), vmem_limit_bytes=VMEM_LIMIT),
        name="ffn",
    )(x, g, wg, wu, wd)


def _memkv_kernel(m_ref, wk_ref, wv_ref, bd_ref, kw_ref, k_ref, v_ref):
    mb = m_ref[...].astype(BF16)
    k_ref[...] = _headnorm_bd(_dot(mb, wk_ref[...]), bd_ref[...], kw_ref[...], MEM_HD)
    v_ref[...] = _dot(mb, wv_ref[...])


def _memkv(mem, wk, wv, kw):
    rows = mem.shape[0]
    bd = _block_diag_ones(MEM_W, MEM_HD)
    return pl.pallas_call(
        _memkv_kernel,
        out_shape=[jax.ShapeDtypeStruct((rows, MEM_W), F32)] * 2,
        compiler_params=pltpu.CompilerParams(vmem_limit_bytes=VMEM_LIMIT),
        name="memkv",
    )(mem, wk, wv, bd, kw)


def _memattn_kernel(n_seq, n, x_ref, g_ref, wq_ref, wo_ref, bd_ref, qw_ref, mk_ref, mv_ref, o_ref, q_sc, a_sc):
    x = x_ref[...]
    h = _rms(x, g_ref[...]).astype(BF16)
    q = _headnorm_bd(_dot(h, wq_ref[...]), bd_ref[...], qw_ref[...], MEM_HD)
    q_sc[...] = q * MEM_HD ** -0.5

    def softmax(sc):
        p = jnp.exp(sc - jnp.max(sc, axis=-1, keepdims=True))
        return (p / jnp.sum(p, axis=-1, keepdims=True)).astype(BF16)

    def one_seq(s, carry):
        r0 = pl.multiple_of(s * n, n)
        qs = q_sc[pl.ds(r0, n), :].astype(BF16)
        mk = mk_ref[s].astype(BF16)
        mv = mv_ref[s].astype(BF16)
        outs = []
        for hh in range(MEM_HEADS):
            sl = slice(hh * MEM_HD, (hh + 1) * MEM_HD)
            outs.append(_dot(softmax(_dot_nt(qs[:, sl], mk[:, sl])), mv[:, sl]))
        a_sc[pl.ds(r0, n), :] = jnp.concatenate(outs, axis=1)
        return carry

    def one_short_seq(own, s, carry):
        r0 = pl.multiple_of(s * n, n)
        qs = jnp.where(own, jnp.concatenate([q_sc[pl.ds(r0, n), :]] * MEM_HEADS, axis=0), 0.0).astype(BF16)
        pv = _dot_nt(softmax(_dot(qs, mk_ref[s].astype(BF16))), mv_ref[s].astype(BF16))
        pv = jnp.where(own, pv, 0.0)
        a_sc[pl.ds(r0, n), :] = sum(pv[hh * n:(hh + 1) * n] for hh in range(MEM_HEADS))
        return carry

    if n_seq > 1:
        own = (lax.broadcasted_iota(jnp.int32, (MEM_HEADS * n, MEM_W), 0) // n
               == lax.broadcasted_iota(jnp.int32, (MEM_HEADS * n, MEM_W), 1) // MEM_HD)
        lax.fori_loop(0, n_seq, functools.partial(one_short_seq, own), 0, unroll=SEQ_UNROLL)
    else:
        one_seq(0, 0)
    o_ref[...] = x + _dot(a_sc[...].astype(BF16), wo_ref[...])


def _memattn(x, g, wq, wo, qw, mk, mv, row0, n_mem, mem0, n_seq, n, steps_per_mem):
    tm = n_seq * n
    steps = n_mem // n_seq * steps_per_mem
    blk0 = row0 // tm
    memblk0 = mem0 // n_seq
    bd = _block_diag_ones(MEM_W, MEM_HD)
    full = lambda a: pl.BlockSpec(a.shape, lambda i: (0,) * a.ndim)
    row_spec = pl.BlockSpec((tm, D_MODEL), lambda i: (blk0 + i, 0))
    mem_spec = pl.BlockSpec((n_seq, MEM_LEN, MEM_W), lambda i: (memblk0 + i // steps_per_mem, 0, 0))
    return pl.pallas_call(
        functools.partial(_memattn_kernel, n_seq, n),
        grid=(steps,),
        in_specs=[row_spec, full(g), full(wq), full(wo), full(bd), full(qw), mem_spec, mem_spec],
        out_specs=row_spec,
        out_shape=jax.ShapeDtypeStruct(x.shape, F32),
        input_output_aliases={0: 0},
        scratch_shapes=[pltpu.VMEM((tm, MEM_W), F32), pltpu.VMEM((tm, MEM_W), F32)],
        compiler_params=pltpu.CompilerParams(dimension_semantics=("arbitrary",), vmem_limit_bytes=VMEM_LIMIT),
        name="memattn",
    )(x, g, wq, wo, bd, qw, mk, mv)


def _sink_softmax(s, sink):
    m = jnp.maximum(jnp.max(s, axis=-1, keepdims=True), sink)
    p = jnp.exp(s - m)
    return p / (jnp.sum(p, axis=-1, keepdims=True) + jnp.exp(sink - m))


def _swa_prompt_kernel(sink_ref, q_ref, kc_ref, kp_ref, vc_ref, vp_ref, bias_ref, o_ref):
    i = pl.program_id(1)
    tq = q_ref.shape[0]
    col = lax.broadcasted_iota(jnp.int32, (tq, 2 * tq), 1)
    first_valid = jnp.where(i > 0, 0, tq)
    colmask = col >= first_valid
    q = q_ref[...]
    outs = []
    for kv in range(ATT_KV):
        sl = slice(kv * HD_A, (kv + 1) * HD_A)
        kb = jnp.concatenate([kp_ref[:, sl], kc_ref[:, sl]], axis=0).astype(BF16)
        vb = jnp.concatenate([vp_ref[:, sl], vc_ref[:, sl]], axis=0).astype(BF16)
        for gg in range(ATT_GROUP):
            hh = kv * ATT_GROUP + gg
            s = _dot_nt(q[:, hh * HD_A:(hh + 1) * HD_A], kb) + bias_ref[hh]
            s = jnp.where(colmask, s, NEG)
            outs.append(_dot(_sink_softmax(s, sink_ref[hh]).astype(BF16), vb))
    o_ref[...] = jnp.concatenate(outs, axis=1).astype(BF16)


def _swa_prompt(sinks, q, k, v, bias, bsz, n):
    tq = WINDOW
    nq = n // tq
    cur = lambda width: pl.BlockSpec((tq, width), lambda b, i: (b * nq + i, 0))
    prev = lambda width: pl.BlockSpec((tq, width), lambda b, i: (b * nq + jnp.maximum(i - 1, 0), 0))
    return pl.pallas_call(
        _swa_prompt_kernel,
        grid=(bsz, nq),
        in_specs=[pl.BlockSpec(memory_space=pltpu.SMEM), cur(E_Q), cur(E_KV), prev(E_KV), cur(E_KV), prev(E_KV),
                  pl.BlockSpec(bias.shape, lambda b, i: (0, 0, 0))],
        out_specs=cur(E_Q),
        out_shape=jax.ShapeDtypeStruct((bsz * n, E_Q), BF16),
        compiler_params=pltpu.CompilerParams(dimension_semantics=("arbitrary", "arbitrary"), vmem_limit_bytes=VMEM_LIMIT),
        name="swa_prompt",
    )(sinks, q, k, k, v, v, bias)


def _swa_sample_kernel(n_seq, q_ref, wk_ref, wv_ref, kn_ref, vn_ref, bias_ref, sink_ref, o_ref):
    rows = q_ref.shape[1]
    pad = jnp.zeros((WINDOW - kn_ref.shape[1], E_KV), F32)
    first = lax.broadcasted_iota(jnp.int32, (rows, HD_A), 0) < rows // ATT_KV
    bias = bias_ref[...]
    sink = sink_ref[...]

    def one_seq(s, carry):
        kall = jnp.concatenate([wk_ref[s], kn_ref[s], pad], axis=0).astype(BF16)
        vall = jnp.concatenate([wv_ref[s], vn_ref[s], pad], axis=0).astype(BF16)
        w = _sink_softmax(_dot_nt(q_ref[s], kall) + bias, sink).astype(BF16)
        pv = _dot(w, vall)
        o_ref[s] = jnp.where(first, pv[:, :HD_A], pv[:, HD_A:])
        return carry

    lax.fori_loop(0, n_seq, one_seq, 0, unroll=SEQ_UNROLL)


def _swa_sample(q, wk, wv, kn, vn, bias, sink_col, n_seq):
    sb, rows, _ = q.shape
    lead = lambda a: pl.BlockSpec((n_seq,) + a.shape[1:], lambda i: (i, 0, 0))
    full = lambda a: pl.BlockSpec(a.shape, lambda i: (0,) * a.ndim)
    return pl.pallas_call(
        functools.partial(_swa_sample_kernel, n_seq),
        grid=(sb // n_seq,),
        in_specs=[lead(q), lead(wk), lead(wv), lead(kn), lead(vn), full(bias), full(sink_col)],
        out_specs=pl.BlockSpec((n_seq, rows, HD_A), lambda i: (i, 0, 0)),
        out_shape=jax.ShapeDtypeStruct((sb, rows, HD_A), F32),
        compiler_params=pltpu.CompilerParams(dimension_semantics=("arbitrary",), vmem_limit_bytes=VMEM_LIMIT),
        name="swa_sample",
    )(q, wk, wv, kn, vn, bias, sink_col)


def _ssd_chunk(xbc, prev, z, dtr, dt_t, hst, cw, cb, dtb, dtb_t, alog, alog_t, dsk, gnw):
    c = xbc.shape[0]
    row = lax.broadcasted_iota(jnp.int32, (c, CONV_DIM), 0)
    conv = cb + xbc * cw[CONV_W - 1:CONV_W]
    for s in range(1, CONV_W):
        shifted = jnp.where(row >= s, pltpu.roll(xbc, s, 0), pltpu.roll(prev, s, 0))
        conv = conv + shifted * cw[CONV_W - 1 - s:CONV_W - s]
    xc = _silu(conv)
    gw = SSM_GROUPS * SSM_N
    xs, bm, cm = xc[:, :SSM_INNER], xc[:, SSM_INNER:SSM_INNER + gw], xc[:, SSM_INNER + gw:]

    lane = lax.broadcasted_iota(jnp.int32, (1, LANES), 1)
    a_row = jnp.where(lane < SSM_HEADS, -jnp.exp(alog), 0.0)
    dt = _softplus(dtr + dtb)
    dta = dt * a_row
    dta_t = _softplus(dt_t + dtb_t) * (-jnp.exp(alog_t))
    ti = lax.broadcasted_iota(jnp.int32, (c, c), 0)
    si = lax.broadcasted_iota(jnp.int32, (c, c), 1)
    tri = ti >= si
    lower = jnp.where(tri, 1.0, 0.0).astype(BF16)
    upper = jnp.where(ti <= si, 1.0, 0.0).astype(BF16)
    cs_col = sum(_dot(lower, part) for part in _split3(dta))
    cs_row = sum(_dot(part, upper) for part in _split3(dta_t))
    ecs = jnp.exp(cs_col)
    cs_last = cs_col[c - 1:c, :]
    tail = jnp.exp(cs_last - cs_col)
    elast = jnp.exp(cs_last)

    ys, hs = [], []
    hpg = SSM_HEADS // SSM_GROUPS
    for g in range(SSM_GROUPS):
        bg = bm[:, g * SSM_N:(g + 1) * SSM_N]
        cgb = cm[:, g * SSM_N:(g + 1) * SSM_N].astype(BF16)
        gmat = _dot_nt(cgb, bg.astype(BF16))
        for hh in range(g * hpg, (g + 1) * hpg):
            seg = cs_col[:, hh:hh + 1] - cs_row[hh:hh + 1, :]
            decay = jnp.exp(jnp.where(tri, seg, NEG))
            xdt = (xs[:, hh * SSM_P:(hh + 1) * SSM_P] * dt[:, hh:hh + 1]).astype(BF16)
            h_h = hst[hh * SSM_P:(hh + 1) * SSM_P, :]
            ys.append(_dot((gmat * decay).astype(BF16), xdt) + _dot_nt(cgb, h_h.astype(BF16)) * ecs[:, hh:hh + 1])
            hs.append(h_h * elast[:, hh:hh + 1] + _dot_tn(xdt, (bg * tail[:, hh:hh + 1]).astype(BF16)))
    y = jnp.concatenate(ys, axis=1) + xs * dsk
    y = y * _silu(z)
    gn = SSM_INNER // SSM_GROUPS
    y = jnp.concatenate([_rms(y[:, g * gn:(g + 1) * gn], gnw[:, g * gn:(g + 1) * gn]) for g in range(SSM_GROUPS)], axis=1)
    return y, jnp.concatenate(hs, axis=0)


def _ssd_prompt_kernel(xbc_ref, prev_ref, z_ref, dt_ref, dtt_ref, cw_ref, cb_ref, dtb_ref, dtbt_ref, alog_ref, alogt_ref,
                       dsk_ref, gnw_ref, y_ref, hl_ref, hs_ref):
    c = pl.program_id(1)

    @pl.when(c == 0)
    def _():
        hs_ref[...] = jnp.zeros(hs_ref.shape, F32)

    prev = prev_ref[...] * (c > 0).astype(F32)
    y, hn = _ssd_chunk(xbc_ref[...], prev, z_ref[...], dt_ref[...], dtt_ref[0], hs_ref[...], cw_ref[...], cb_ref[...],
                       dtb_ref[...], dtbt_ref[...], alog_ref[...], alogt_ref[...], dsk_ref[...], gnw_ref[...])
    y_ref[...] = y.astype(BF16)
    hs_ref[...] = hn
    hl_ref[0] = hn


def _ssd_prompt(xbc, z, dt, dt_t, params, bsz, n):
    c = SSD_CHUNK
    nc = n // c
    cur = lambda width: pl.BlockSpec((c, width), lambda b, i: (b * nc + i, 0))
    full = lambda a: pl.BlockSpec(a.shape, lambda b, i: (0,) * a.ndim)
    hd = SSM_HEADS * SSM_P
    return pl.pallas_call(
        _ssd_prompt_kernel,
        grid=(bsz, nc),
        in_specs=[cur(CONV_DIM), pl.BlockSpec((c, CONV_DIM), lambda b, i: (b * nc + jnp.maximum(i - 1, 0), 0)),
                  cur(SSM_INNER), cur(LANES), pl.BlockSpec((1, SSM_HEADS, c), lambda b, i: (b, 0, i))]
        + [full(p) for p in params],
        out_specs=[cur(SSM_INNER), pl.BlockSpec((1, hd, SSM_N), lambda b, i: (b, 0, 0))],
        out_shape=[jax.ShapeDtypeStruct((bsz * n, SSM_INNER), BF16), jax.ShapeDtypeStruct((bsz, hd, SSM_N), F32)],
        scratch_shapes=[pltpu.VMEM((hd, SSM_N), F32)],
        compiler_params=pltpu.CompilerParams(dimension_semantics=("arbitrary", "arbitrary"), vmem_limit_bytes=VMEM_LIMIT),
        name="ssd_prompt",
    )(xbc, xbc, z, dt, dt_t, *params)


def _ssd_sample_kernel(n_seq, xbc_ref, st_ref, z_ref, dt_ref, dtt_ref, h0_ref, cw_ref, cb_ref, dtb_ref, dtbt_ref,
                       alog_ref, alogt_ref, dsk_ref, gnw_ref, y_ref, hn_ref):
    def one_seq(s, carry):
        y, hn = _ssd_chunk(xbc_ref[s], st_ref[s], z_ref[s], dt_ref[s], dtt_ref[s], h0_ref[s], cw_ref[...], cb_ref[...],
                           dtb_ref[...], dtbt_ref[...], alog_ref[...], alogt_ref[...], dsk_ref[...], gnw_ref[...])
        y_ref[s] = y
        hn_ref[s] = hn
        return carry

    lax.fori_loop(0, n_seq, one_seq, 0)


def _ssd_sample(xbc, st, z, dt, dt_t, h0, params, n_seq):
    sb = xbc.shape[0]
    lead = lambda a: pl.BlockSpec((n_seq,) + a.shape[1:], lambda i: (i, 0, 0))
    full = lambda a: pl.BlockSpec(a.shape, lambda i: (0,) * a.ndim)
    seq = (xbc, st, z, dt, dt_t, h0)
    return pl.pallas_call(
        functools.partial(_ssd_sample_kernel, n_seq),
        grid=(sb // n_seq,),
        in_specs=[lead(a) for a in seq] + [full(p) for p in params],
        out_specs=[lead(z), lead(h0)],
        out_shape=[jax.ShapeDtypeStruct(z.shape, F32), jax.ShapeDtypeStruct(h0.shape, F32)],
        compiler_params=pltpu.CompilerParams(dimension_semantics=("arbitrary",), vmem_limit_bytes=VMEM_LIMIT),
        name="ssd_sample",
    )(*seq, *params)


def _moba_prompt_kernel(q_ref, k_ref, v_ref, km_ref, down_ref, dprev_ref, cfar_ref, o_ref,
                        qa_ref, s_ref, p_ref, m_ref, al_ref, acc_ref):
    cur = pl.program_id(2)
    tq = q_ref.shape[0]
    rows = ATT_GROUP * tq
    qs = jnp.concatenate([q_ref[:, g * HD_C:(g + 1) * HD_C] for g in range(ATT_GROUP)], axis=0)
    lane = lax.broadcasted_iota(jnp.int32, (rows, LANES), 1)
    gate = jnp.where(lane < cur, _dot_nt(qs, km_ref[0, 0]), NEG)
    sel = _top3_mask(gate, lane.astype(F32))
    allowed = jnp.where(lane < cur, sel, jnp.where(lane == cur, 1.0, 0.0))
    c = cfar_ref[0]
    c_hi = c.astype(BF16)
    c_lo = (c - c_hi.astype(F32)).astype(BF16)
    upper = jnp.where(lane == CONST_LANE, c_hi, jnp.where(lane == CONST_LANE + 1, c_lo,
                                                         jnp.where(allowed > 0.5, 0.0, NEG).astype(BF16)))
    qa_ref[:, :HD_C] = qs
    qa_ref[:, HD_C:] = upper
    m_ref[...] = jnp.full(m_ref.shape, LOWEST, F32)
    acc_ref[...] = jnp.zeros(acc_ref.shape, F32)

    groups = [slice(i * rows // MXU_ROW_GROUPS, (i + 1) * rows // MXU_ROW_GROUPS) for i in range(MXU_ROW_GROUPS)]

    def score(j, slot):
        r0 = pl.multiple_of(j * MOBA_BLOCK, MOBA_BLOCK)
        kblk = k_ref[pl.ds(r0, MOBA_BLOCK), :]
        for gs in groups:
            s_ref[slot, gs, :] = _dot_nt(qa_ref[gs, :], kblk)

    def consume(j, slot, delta_ref):
        for r in range(rows // SUB_ROWS):
            sl = slice(r * SUB_ROWS, (r + 1) * SUB_ROWS)
            halves = [s_ref[slot, sl, hf * LANES:(hf + 1) * LANES] for hf in range(2)]
            if delta_ref is not None:
                halves = [sh + delta_ref[0, sl, hf * LANES:(hf + 1) * LANES] for hf, sh in enumerate(halves)]
            m_old = m_ref[sl, :]
            m_new = jnp.maximum(m_old, jnp.max(jnp.maximum(halves[0], halves[1]), axis=-1, keepdims=True))
            for hf, sh in enumerate(halves):
                p_ref[sl, hf * LANES:(hf + 1) * LANES] = jnp.exp2(sh - m_new).astype(BF16)
            al_ref[sl, :] = jnp.exp2(m_old - m_new)
            m_ref[sl, :] = m_new
        r0 = pl.multiple_of(j * MOBA_BLOCK, MOBA_BLOCK)
        vblk = v_ref[pl.ds(r0, MOBA_BLOCK), :]
        for gs in groups:
            acc_ref[gs, :] = jnp.concatenate([al_ref[gs, :]] * 2, axis=1) * acc_ref[gs, :] + _dot(p_ref[gs, :], vblk)

    n_far = jnp.maximum(cur - 1, 0)
    last = jnp.maximum(n_far - 1, 0)
    score(cur, 0)

    @pl.when(cur == 0)
    def _():
        consume(cur, 0, down_ref)

    @pl.when(cur >= 1)
    def _():
        score(cur - 1, 1)
        consume(cur, 0, down_ref)
        score(0, 0)
        consume(cur - 1, 1, dprev_ref)

    def far_pair(t, carry):
        j = 2 * t
        score(jnp.minimum(j + 1, last), 1)
        consume(j, 0, None)

        @pl.when(j + 1 < n_far)
        def _():
            score(jnp.minimum(j + 2, last), 0)
            consume(j + 1, 1, None)

        return carry

    lax.fori_loop(0, (n_far + 1) // 2, far_pair, 0)
    o = acc_ref[:, :HD_C] / acc_ref[:, HD_C:]
    for g in range(ATT_GROUP):
        o_ref[:, g * HD_C:(g + 1) * HD_C] = o[g * tq:(g + 1) * tq].astype(BF16)


def _moba_prompt(q, kaug, vaug, kmean, down, dprev, cfar, bsz, n):
    tq = MOBA_BLOCK
    nq = n // tq
    assert nq <= CONST_LANE
    rows = ATT_GROUP * tq
    gw = ATT_GROUP * HD_C
    qspec = pl.BlockSpec((tq, gw), lambda b, kv, i: (b * nq + i, kv))
    kvspec = pl.BlockSpec((n, 2 * HD_C), lambda b, kv, i: (b, kv))
    tab = lambda a: pl.BlockSpec((1,) + a.shape[1:], lambda b, kv, i: (kv, 0, 0))
    return pl.pallas_call(
        _moba_prompt_kernel,
        grid=(bsz, ATT_KV, nq),
        in_specs=[qspec, kvspec, kvspec, pl.BlockSpec((1, 1, LANES, HD_C), lambda b, kv, i: (b, kv, 0, 0)),
                  tab(down), tab(dprev), tab(cfar)],
        out_specs=qspec,
        out_shape=jax.ShapeDtypeStruct((bsz * n, C_Q), BF16),
        scratch_shapes=[pltpu.VMEM((rows, 2 * HD_C), BF16), pltpu.VMEM((2, rows, MOBA_BLOCK), F32),
                        pltpu.VMEM((rows, MOBA_BLOCK), BF16), pltpu.VMEM((rows, LANES), F32),
                        pltpu.VMEM((rows, LANES), F32), pltpu.VMEM((rows, 2 * HD_C), F32)],
        compiler_params=pltpu.CompilerParams(dimension_semantics=("arbitrary",) * 3, vmem_limit_bytes=VMEM_LIMIT),
        name="moba_prompt",
    )(q, kaug, vaug, kmean, down, dprev, cfar)


def _moba_sample_kernel(nbs, ppb, pt_ref, q_ref, *refs):
    npg = nbs * ppb
    k_pages, v_pages = refs[:npg], refs[npg:2 * npg]
    kn_ref, vn_ref, tbl_ref, bown_ref, o_ref, gate_sc, m_sc, l_sc, o_sc = refs[2 * npg:]
    step = pl.program_id(1)
    nblk = pl.num_programs(1) * nbs
    rows = q_ref.shape[1]
    lane = lax.broadcasted_iota(jnp.int32, (rows, LANES), 1)

    @pl.when(step == 0)
    def _():
        gate_sc[...] = jnp.zeros(gate_sc.shape, F32)
        m_sc[...] = jnp.zeros(m_sc.shape, F32)
        l_sc[...] = jnp.zeros(l_sc.shape, F32)

    q = q_ref[0]
    half = rows // ATT_KV
    lane_h = lax.broadcasted_iota(jnp.int32, (half, LANES), 1)
    own = lambda pages, kv: jnp.concatenate([r[pl.ds(kv, PAGE_SIZE, stride=ATT_KV), :] for r in pages], axis=0)
    gp = SCORE_BLOCKS * ppb
    units = [(kv, g) for g in range(nbs // SCORE_BLOCKS) for kv in range(ATT_KV)]
    keys = [own(k_pages[g * gp:(g + 1) * gp], kv) for kv, g in units]
    scores = [_dot_nt(q[kv * half:(kv + 1) * half], kk.astype(BF16)) for (kv, g), kk in zip(units, keys)]
    for (kv, g), kk, s_all in zip(units, keys, scores):
        rs = slice(kv * half, (kv + 1) * half)
        vv = own(v_pages[g * gp:(g + 1) * gp], kv)
        qf = q[rs].astype(F32)
        gate_n, m_n, l_n = gate_sc[rs, :], m_sc[rs, :], l_sc[rs, :]
        for i in range(SCORE_BLOCKS):
            j = step * nbs + g * SCORE_BLOCKS + i
            cs = slice(i * MOBA_BLOCK, (i + 1) * MOBA_BLOCK)
            kmean = (jnp.sum(kk[cs], axis=0, keepdims=True) * (1.0 / MOBA_BLOCK)).astype(BF16).astype(F32)
            gate = jnp.sum(qf * kmean, axis=-1, keepdims=True)
            s = s_all[:, cs] + tbl_ref[jnp.where(j == nblk - 1, 1, 0), rs, :]
            m = jnp.max(s, axis=-1, keepdims=True)
            p = jnp.exp2(s - m)
            hit = lane_h == j
            gate_n = jnp.where(hit, gate, gate_n)
            m_n = jnp.where(hit, m, m_n)
            l_n = jnp.where(hit, jnp.sum(p, axis=-1, keepdims=True), l_n)
            o_sc[j, rs, :] = _dot(p.astype(BF16), vv[cs].astype(BF16))
        gate_sc[rs, :], m_sc[rs, :], l_sc[rs, :] = gate_n, m_n, l_n

    @pl.when(step == pl.num_programs(1) - 1)
    def _():
        pad = jnp.zeros((LANES - kn_ref.shape[1], HD_C), F32)
        sel = _top3_mask(jnp.where(lane < nblk, gate_sc[...], LOWEST), lane.astype(F32)) > 0.5
        m_blk = jnp.where(sel, m_sc[...], LOWEST)
        kn = jnp.concatenate([kn_ref[0], pad], axis=0).astype(BF16)
        vn = jnp.concatenate([vn_ref[0], pad], axis=0).astype(BF16)
        s_own = _dot_nt(q, kn) + bown_ref[...]
        m_tot = jnp.maximum(jnp.max(m_blk, axis=-1, keepdims=True), jnp.max(s_own, axis=-1, keepdims=True))
        w = jnp.exp2(m_blk - m_tot)
        p_own = jnp.exp2(s_own - m_tot)
        den = jnp.sum(w * l_sc[...], axis=-1, keepdims=True) + jnp.sum(p_own, axis=-1, keepdims=True)
        o = _dot(p_own.astype(BF16), vn)
        for b in range(o_sc.shape[0]):
            o = o + w[:, b:b + 1] * o_sc[b]
        o_ref[0] = o / den


def _moba_sample(pt, q, pool_k, pool_v, kn, vn, tbl, bown, n_pages):
    sb, rows, _ = q.shape
    ppb = MOBA_BLOCK // PAGE_SIZE
    nblk = n_pages // ppb
    nbs = SAMPLE_BLOCKS_PER_STEP
    assert nblk % nbs == 0 and nblk <= LANES

    def page(p):
        return pl.BlockSpec((PAGE_SIZE * ATT_KV, HD_C), lambda b, s, pt_ref: (pt_ref[b * n_pages + s * nbs * ppb + p], 0))

    seq = lambda a: pl.BlockSpec((1,) + a.shape[1:], lambda b, s, pt_ref: (b,) + (0,) * (a.ndim - 1))
    full = lambda a: pl.BlockSpec(a.shape, lambda b, s, pt_ref: (0,) * a.ndim)
    pages = [page(p) for p in range(nbs * ppb)]
    grid_spec = pltpu.PrefetchScalarGridSpec(
        num_scalar_prefetch=1,
        grid=(sb, nblk // nbs),
        in_specs=[seq(q)] + pages * 2 + [seq(kn), seq(vn), full(tbl), full(bown)],
        out_specs=seq(q),
        scratch_shapes=[pltpu.VMEM((rows, LANES), F32)] * 3 + [pltpu.VMEM((nblk, rows, HD_C), F32)],
    )
    return pl.pallas_call(
        functools.partial(_moba_sample_kernel, nbs, ppb),
        grid_spec=grid_spec,
        out_shape=jax.ShapeDtypeStruct(q.shape, F32),
        compiler_params=pltpu.CompilerParams(dimension_semantics=("arbitrary", "arbitrary"), vmem_limit_bytes=VMEM_LIMIT),
        name="moba_sample",
    )(pt, q, *([pool_k] * len(pages)), *([pool_v] * len(pages)), kn, vn, tbl, bown)


def _bias_table(tab_t, head, dist, valid):
    rows_tab = jnp.take(tab_t, jnp.asarray(np.asarray(head, np.int32)), axis=0)
    bucket = jnp.asarray(_bucket_np(np.broadcast_to(dist, np.broadcast_shapes(np.shape(dist), np.shape(valid)))))
    out = jnp.zeros(bucket.shape, F32)
    for b in range(N_BUCKETS):
        out = jnp.where(bucket == b, rows_tab[:, b:b + 1], out)
    return jnp.where(jnp.asarray(np.broadcast_to(valid, bucket.shape)), out, NEG)


def _head_rows(per_head):
    return np.repeat(np.arange(ATT_HEADS), per_head)


def _to_group_rows(a, sb, sn, hd):
    return a.reshape(sb, sn, ATT_KV, ATT_GROUP, hd).transpose(0, 2, 3, 1, 4).reshape(sb, ATT_KV, ATT_GROUP * sn, hd)


def _from_group_rows(a, sb, sn, hd):
    return a.reshape(sb, ATT_KV, ATT_GROUP, sn, hd).transpose(0, 3, 1, 2, 4).reshape(sb * sn, ATT_HEADS * hd)


def _pad_lanes(v):
    return jnp.pad(v.astype(F32), (0, LANES - v.shape[0]))[None]


def kernel(x_prompt, x_sample, cache_win_k, cache_win_v, state_conv, state_ssm, cache_moba_k, cache_moba_v, cache_mem_k, cache_mem_v, page_table, mem_prompt, rel_bias, norm_mix, norm_mem, norm_ffn, even_w_in, even_w_out, even_q_norm, even_k_norm, even_sinks, even_conv_w, even_conv_b, even_dt_bias, even_a_log, even_d_skip, even_gate_norm, odd_w_in, odd_w_out, odd_q_norm, odd_k_norm, mem_wq, mem_wk, mem_wv, mem_wo, mem_q_norm, mem_k_norm, ffn_w_gate, ffn_w_up, ffn_w_down):
    bsz, n = x_prompt.shape[:2]
    sb, sn = x_sample.shape[:2]
    depth = norm_mix.shape[0]
    rp, rs = bsz * n, sb * sn
    n_pages = page_table.shape[1]
    n_pool = cache_moba_k.shape[1]
    past_len = n_pages * PAGE_SIZE
    assert past_len % MOBA_BLOCK == 0 and cache_win_k.shape[2] == WINDOW and sn >= CONV_W - 1
    seq_group = 16

    x = jnp.concatenate([x_prompt.reshape(rp, D_MODEL), x_sample.reshape(rs, D_MODEL)], axis=0)
    tab_t = rel_bias.T.astype(F32)

    qi = np.tile(np.arange(WINDOW), ATT_HEADS)[:, None]
    kj = np.arange(2 * WINDOW)[None, :]
    d = WINDOW + qi - kj
    swa_bias_p = _bias_table(tab_t, _head_rows(WINDOW), d, (d >= 0) & (d < WINDOW)).reshape(ATT_HEADS, WINDOW, 2 * WINDOW)
    ti = np.tile(np.arange(sn), ATT_HEADS)[:, None]
    d = np.where(kj < WINDOW, WINDOW + ti - kj, ti - (kj - WINDOW))
    swa_bias_s = _bias_table(tab_t, _head_rows(sn), d, (d >= 0) & (d < WINDOW) & (kj < WINDOW + sn))
    assert _bucket_np(MOBA_BLOCK + 1) == N_BUCKETS - 1
    far_bias = tab_t[:, N_BUCKETS - 1] * LOG2E
    qi = np.tile(np.arange(MOBA_BLOCK), ATT_HEADS)[:, None]
    kj = np.arange(MOBA_BLOCK)[None, :]
    heads_p = _head_rows(MOBA_BLOCK)
    cfar_p = jnp.take(far_bias, jnp.asarray(heads_p))[:, None]
    grp = lambda t: t.reshape(ATT_KV, ATT_GROUP * MOBA_BLOCK, t.shape[-1])
    down_p = grp(jnp.where(jnp.asarray(qi >= kj), _bias_table(tab_t, heads_p, qi - kj, True) * LOG2E - cfar_p, NEG))
    dprev_p = grp(_bias_table(tab_t, heads_p, MOBA_BLOCK + qi - kj, True) * LOG2E - cfar_p)
    cfar_p = grp(jnp.broadcast_to(cfar_p, (cfar_p.shape[0], LANES)))
    heads_s = _head_rows(sn)
    row_kv = (heads_s // ATT_GROUP)[:, None]
    cfar_s = jnp.take(far_bias, jnp.asarray(heads_s))[:, None]
    tbl_s = jnp.stack([jnp.broadcast_to(cfar_s, (cfar_s.shape[0], MOBA_BLOCK)),
                       _bias_table(tab_t, heads_s, MOBA_BLOCK + ti - kj, True) * LOG2E])
    col = np.arange(LANES)[None, :]
    tok = col // ATT_KV
    bown_s = _bias_table(tab_t, heads_s, ti - tok, (row_kv == col % ATT_KV) & (ti >= tok) & (tok < sn)) * LOG2E

    pool_k = cache_moba_k.reshape(-1, HD_C)
    pool_v = cache_moba_v.reshape(-1, HD_C)
    pt_flat = page_table.reshape(-1)
    mem_k = cache_mem_k.transpose(0, 1, 3, 4, 2).reshape(depth * sb, MEM_W, MEM_LEN)
    mem_v = cache_mem_v.transpose(0, 1, 3, 4, 2).reshape(depth * sb, MEM_W, MEM_LEN)
    mem_rows = mem_prompt.reshape(bsz * MEM_LEN, D_MODEL)

    pwk, pwv, pcv, pss, pmk, pmv, pxk, pxv = [], [], [], [], [], [], [], []
    swk, swv, scv, sss, smk, smv = [], [], [], [], [], []
    for l in range(depth):
        g_mix = norm_mix[l][None]
        if l % 2 == 0:
            e = l // 2
            w_in = jnp.pad(even_w_in[e], ((0, 0), (0, EVEN_W - even_w_in.shape[2]))).astype(BF16)
            w_out = even_w_out[e].astype(BF16)
            qw = jnp.tile(even_q_norm[e], ATT_HEADS)[None]
            kw = jnp.tile(even_k_norm[e], ATT_KV)[None]
            q, k, v, z, xbc, dt = _proj_even(x, g_mix, w_in, qw, kw)
            params = (even_conv_w[e], even_conv_b[e][None], _pad_lanes(even_dt_bias[e]), even_dt_bias[e][:, None],
                      _pad_lanes(even_a_log[e]), even_a_log[e][:, None], jnp.repeat(even_d_skip[e], SSM_P)[None],
                      even_gate_norm[e][None])
            att_p = _swa_prompt(even_sinks[e], q, k, v, swa_bias_p, bsz, n)
            dt_t = dt[:rp, :SSM_HEADS].reshape(bsz, n, SSM_HEADS).transpose(0, 2, 1)
            y_p, h_p = _ssd_prompt(xbc, z, dt, dt_t, params, bsz, n)
            sink_col = jnp.repeat(even_sinks[e], sn)[:, None]
            k_s = k[rp:].reshape(sb, sn, E_KV)
            v_s = v[rp:].reshape(sb, sn, E_KV)
            qg = _to_group_rows(q[rp:], sb, sn, HD_A)
            zq = jnp.zeros_like(qg[:, 0])
            q_bd = jnp.concatenate([jnp.concatenate([qg[:, 0], zq], axis=-1),
                                    jnp.concatenate([zq, qg[:, 1]], axis=-1)], axis=1)
            att_s = _swa_sample(q_bd, cache_win_k[e].reshape(sb, WINDOW, E_KV), cache_win_v[e].reshape(sb, WINDOW, E_KV),
                                k_s, v_s, swa_bias_s, sink_col, seq_group)
            xbc_s = xbc[rp:].reshape(sb, sn, CONV_DIM)
            st = jnp.pad(state_conv[e], ((0, 0), (sn - (CONV_W - 1), 0), (0, 0)))
            dt_s = dt[rp:].reshape(sb, sn, LANES)
            y_s, h_s = _ssd_sample(xbc_s, st, z[rp:].reshape(sb, sn, SSM_INNER), dt_s,
                                   dt_s[:, :, :SSM_HEADS].transpose(0, 2, 1),
                                   state_ssm[e].reshape(sb, SSM_HEADS * SSM_P, SSM_N), params, seq_group)
            att = jnp.concatenate([att_p, _from_group_rows(att_s, sb, sn, HD_A).astype(BF16)], axis=0)
            yy = jnp.concatenate([y_p, y_s.reshape(rs, SSM_INNER).astype(BF16)], axis=0)
            x = _outproj(x, [att, yy], [w_out[:E_Q], w_out[E_Q:]])
            tail = lambda a, cnt: jnp.stack([a[(b + 1) * n - cnt:(b + 1) * n] for b in range(bsz)])
            pwk.append(tail(k, WINDOW).reshape(bsz, WINDOW, ATT_KV, HD_A))
            pwv.append(tail(v, WINDOW).reshape(bsz, WINDOW, ATT_KV, HD_A))
            pcv.append(tail(xbc, CONV_W - 1))
            pss.append(h_p.reshape(bsz, SSM_HEADS, SSM_P, SSM_N))
            swk.append(jnp.concatenate([cache_win_k[e], k_s.reshape(sb, sn, ATT_KV, HD_A)], axis=1)[:, -WINDOW:])
            swv.append(jnp.concatenate([cache_win_v[e], v_s.reshape(sb, sn, ATT_KV, HD_A)], axis=1)[:, -WINDOW:])
            scv.append(jnp.concatenate([state_conv[e], xbc_s], axis=1)[:, -(CONV_W - 1):])
            sss.append(h_s.reshape(sb, SSM_HEADS, SSM_P, SSM_N))
        else:
            o = l // 2
            w_in = odd_w_in[o].astype(BF16)
            w_out = odd_w_out[o].astype(BF16)
            nb = n // MOBA_BLOCK
            q, k, v, kaug, vaug, km = _proj_odd(x, g_mix, w_in, odd_q_norm[o][None], odd_k_norm[o][None], nb)
            kmean = km[:bsz * nb, 0].reshape(bsz, nb, ATT_KV, HD_C).transpose(0, 2, 1, 3)
            kmean = jnp.pad(kmean, ((0, 0), (0, 0), (0, LANES - nb), (0, 0))).astype(BF16)
            att_p = _moba_prompt(q, kaug, vaug, kmean, down_p, dprev_p, cfar_p, bsz, n)
            k_s = k[rp:].reshape(sb, sn, C_KV)
            v_s = v[rp:].reshape(sb, sn, C_KV)
            q_s = _to_group_rows(q[rp:], sb, sn, HD_C).reshape(sb, ATT_HEADS * sn, HD_C)
            att_s = _moba_sample(pt_flat + o * n_pool, q_s, pool_k, pool_v, k_s.reshape(sb, sn * ATT_KV, HD_C),
                                 v_s.reshape(sb, sn * ATT_KV, HD_C), tbl_s, bown_s, n_pages)
            att = jnp.concatenate([att_p, _from_group_rows(att_s, sb, sn, HD_C).astype(BF16)], axis=0)
            x = _outproj(x, [att], [w_out])
            pmk.append(k[:rp].reshape(bsz, n, ATT_KV, HD_C))
            pmv.append(v[:rp].reshape(bsz, n, ATT_KV, HD_C))
            smk.append(k_s.reshape(sb, sn, ATT_KV, HD_C))
            smv.append(v_s.reshape(sb, sn, ATT_KV, HD_C))
        mqw = jnp.tile(mem_q_norm[l], MEM_HEADS)[None]
        mkw = jnp.tile(mem_k_norm[l], MEM_HEADS)[None]
        wq = mem_wq[l].astype(BF16)
        wo = mem_wo[l].astype(BF16)
        xk, xv = _memkv(mem_rows, mem_wk[l].astype(BF16), mem_wv[l].astype(BF16), mkw)
        pxk.append(xk.reshape(bsz, MEM_LEN, MEM_HEADS, MEM_HD))
        pxv.append(xv.reshape(bsz, MEM_LEN, MEM_HEADS, MEM_HD))
        g_mem = norm_mem[l][None]
        x = _memattn(x, g_mem, wq, wo, mqw, xk.reshape(bsz, MEM_LEN, MEM_W), xv.reshape(bsz, MEM_LEN, MEM_W),
                     0, bsz, 0, 1, ROW_TILE, n // ROW_TILE)
        x = _memattn(x, g_mem, wq, wo, mqw, mem_k, mem_v, rp, sb, l * sb, seq_group, sn, 1)
        x = _ffn(x, norm_ffn[l][None], ffn_w_gate[l].astype(BF16), ffn_w_up[l].astype(BF16), ffn_w_down[l].astype(BF16))

    return (x[:rp].reshape(bsz, n, D_MODEL), x[rp:].reshape(sb, sn, D_MODEL),
            jnp.stack(pwk), jnp.stack(pwv), jnp.stack(pcv), jnp.stack(pss), jnp.stack(pmk), jnp.stack(pmv),
            jnp.stack(pxk), jnp.stack(pxv), jnp.stack(swk), jnp.stack(swv), jnp.stack(scv), jnp.stack(sss),
            jnp.stack(smk), jnp.stack(smv))
```
